```python
import jax, jax.numpy as jnp
from jax import lax
import numpy as np

D_MODEL = 1024
BATCH = 8
SEQ = 4096
DEPTH = 2

CTX_LEN = 256
GRID_W = 64

POOL_WINDOWS = (2, 4, 8, 16)
POOL_GROUPS = len(POOL_WINDOWS)
POOL_GROUP_DIM = D_MODEL // 8
POOL_DIM = POOL_GROUPS * POOL_GROUP_DIM
MLA_HEADS = 8
QK_NOPE = 64
QK_ROPE = 32
V_DIM = 64
Q_LORA = D_MODEL // 4
KV_LORA = D_MODEL // 8
QK_DIM = QK_NOPE + QK_ROPE
KV_OFF = POOL_DIM + Q_LORA
MIX_IN = POOL_DIM + Q_LORA + KV_LORA + QK_ROPE
MIX_OUT = POOL_DIM + MLA_HEADS * V_DIM
ROPE_THETA = 10000.0
Q_BLOCK = 128

CONV_WIDTH = 31

N_EXPERTS = 16
EXPERT_FF = D_MODEL
CAPACITY_FACTOR = 2

EPS = 1e-6
N_EVEN = (DEPTH + 1) // 2
N_ODD = DEPTH // 2

kernel_name = "hybrid_pool_mla_conformer_ecmoe_dit"


def rms_norm(x, g):
    xf = x.astype(jnp.float32)
    y = xf * lax.rsqrt(jnp.mean(xf * xf, axis=-1, keepdims=True) + EPS)
    return (y * g.astype(jnp.float32)).astype(x.dtype)


def layer_norm(x, g, b):
    xf = x.astype(jnp.float32)
    mu = jnp.mean(xf, axis=-1, keepdims=True)
    var = jnp.mean(jnp.square(xf - mu), axis=-1, keepdims=True)
    y = (xf - mu) * lax.rsqrt(var + EPS)
    return (y * g.astype(jnp.float32) + b.astype(jnp.float32)).astype(x.dtype)


def modulate(h, shift, scale):
    return h * (1 + scale) + shift


def axial_rope_tables(n_tokens):
    rows = n_tokens // GRID_W
    row = jnp.repeat(jnp.arange(rows), GRID_W).astype(jnp.float32)
    col = jnp.tile(jnp.arange(GRID_W), rows).astype(jnp.float32)
    per_axis = QK_ROPE // 2
    inv_freq = 1.0 / (ROPE_THETA ** (jnp.arange(0, per_axis, 2, dtype=jnp.float32) / per_axis))
    ang = jnp.stack([row[:, None] * inv_freq, col[:, None] * inv_freq], axis=1)
    return jnp.cos(ang), jnp.sin(ang)


def apply_rope(x, cos, sin):
    xr = x.reshape(x.shape[:-1] + (2, 2, QK_ROPE // 4)).astype(jnp.float32)
    x1, x2 = xr[..., 0, :], xr[..., 1, :]
    out = jnp.stack([x1 * cos - x2 * sin, x1 * sin + x2 * cos], axis=-2)
    return out.reshape(x.shape).astype(x.dtype)


def multiscale_pool(u, pool_w, pool_scale):
    b, n, _ = u.shape
    uf = u.astype(jnp.float32)
    csum = jnp.concatenate([jnp.zeros_like(uf[:, :1]), jnp.cumsum(uf, axis=1)], axis=1)
    t = jnp.arange(n)
    outs = []
    for g, w in enumerate(POOL_WINDOWS):
        lo = jnp.clip(t - w // 2, 0, n)
        hi = jnp.clip(t - w // 2 + w, 0, n)
        sl = slice(g * POOL_GROUP_DIM, (g + 1) * POOL_GROUP_DIM)
        cg = csum[..., sl]
        mean = (jnp.take(cg, hi, axis=1) - jnp.take(cg, lo, axis=1)) / (hi - lo).astype(jnp.float32)[:, None]
        outs.append(mean - uf[..., sl])
    d = jnp.stack(outs, axis=2).astype(u.dtype)
    y = jnp.einsum('bngc,gcd->bngd', d, pool_w).reshape(b, n, POOL_DIM)
    return y * pool_scale


def mla_q(zq, q_norm_g, w_uq, rope):
    b, n, _ = zq.shape
    q = (rms_norm(zq, q_norm_g) @ w_uq).reshape(b, n, MLA_HEADS, QK_DIM)
    q_nope, q_rope = q[..., :QK_NOPE], q[..., QK_NOPE:]
    if rope is not None:
        cos, sin = rope
        q_rope = apply_rope(q_rope, cos[:, None], sin[:, None])
    return jnp.concatenate([q_nope, q_rope], axis=-1)


def mla_kv(zkv, kv_norm_g, w_ukv, rope):
    b, n, _ = zkv.shape
    kv = (rms_norm(zkv[..., :KV_LORA], kv_norm_g) @ w_ukv).reshape(b, n, MLA_HEADS, QK_NOPE + V_DIM)
    k_nope, v = kv[..., :QK_NOPE], kv[..., QK_NOPE:]
    k_rope = zkv[..., KV_LORA:]
    if rope is not None:
        cos, sin = rope
        k_rope = apply_rope(k_rope, cos, sin)
    k_rope = jnp.broadcast_to(k_rope[:, :, None, :], (b, n, MLA_HEADS, QK_ROPE))
    return jnp.concatenate([k_nope, k_rope], axis=-1), v


def latent_attention(q, k, v, k_ctx, v_ctx):
    b, n, h, dk = q.shape
    scale = QK_DIM ** -0.5
    lc = k_ctx.shape[1]
    qb = q.reshape(b, n // Q_BLOCK, Q_BLOCK, h, dk).swapaxes(0, 1)

    def block(qblk):
        s = jnp.concatenate([jnp.einsum('bqhd,bkhd->bhqk', qblk, k_ctx),
                             jnp.einsum('bqhd,bkhd->bhqk', qblk, k)], axis=-1)
        p = jax.nn.softmax(s.astype(jnp.float32) * scale, axis=-1).astype(v.dtype)
        return (jnp.einsum('bhqk,bkhd->bqhd', p[..., :lc], v_ctx)
                + jnp.einsum('bhqk,bkhd->bqhd', p[..., lc:], v))

    o = lax.map(block, qb)
    return o.swapaxes(0, 1).reshape(b, n, h * V_DIM)


def context_attention(q, k, v):
    b, n = q.shape[:2]
    s = jnp.einsum('bqhd,bkhd->bhqk', q, k).astype(jnp.float32) * (QK_DIM ** -0.5)
    p = jax.nn.softmax(s, axis=-1).astype(v.dtype)
    return jnp.einsum('bhqk,bkhd->bqhd', p, v).reshape(b, n, MLA_HEADS * V_DIM)


def conformer_conv(h, w_pw1, b_pw1, w_dw, b_dw, ln_g, ln_b, w_pw2, b_pw2):
    a, g = jnp.split(h @ w_pw1 + b_pw1, 2, axis=-1)
    u = a * jax.nn.sigmoid(g)
    u = lax.conv_general_dilated(u, w_dw[:, None, :], window_strides=(1,),
                                 padding=[(CONV_WIDTH // 2, CONV_WIDTH // 2)],
                                 dimension_numbers=('NWC', 'WIO', 'NWC'),
                                 feature_group_count=D_MODEL) + b_dw
    u = jax.nn.silu(layer_norm(u, ln_g, ln_b))
    return u @ w_pw2 + b_pw2


def expert_choice_moe(h, router_w, wg, wu, wd):
    b, n, d = h.shape
    cap = CAPACITY_FACTOR * n // N_EXPERTS
    aff = jax.nn.softmax((h @ router_w).astype(jnp.float32), axis=-1)
    gate, idx = lax.top_k(aff.swapaxes(1, 2), cap)
    xs = jax.vmap(lambda hb, ib: hb[ib])(h, idx)
    a = jnp.einsum('becd,edf->becf', xs, wg)
    u = jnp.einsum('becd,edf->becf', xs, wu)
    y = jnp.einsum('becf,efd->becd', jax.nn.silu(a) * u, wd) * gate[..., None].astype(h.dtype)
    return jax.vmap(lambda yb, ib: jnp.zeros((n, d), h.dtype).at[ib.reshape(-1)].add(yb.reshape(-1, d)))(y, idx)


def setup_inputs(seed: int = 0) -> dict:
    key = jax.random.key(seed)
    ks = jax.random.split(key, 32)
    nrm = lambda k, shape, s: jax.random.normal(k, shape, jnp.float32) * s
    D, L = D_MODEL, DEPTH
    return {
        "x": nrm(ks[0], (BATCH, SEQ, D), 1.0),
        "c": nrm(ks[1], (BATCH, D), 1.0),
        "ctx": nrm(ks[2], (BATCH, CTX_LEN, D), 1.0),
        "c_ctx": nrm(ks[3], (D,), 1.0),
        "ada_w": nrm(ks[4], (L, D, 6 * D), 0.5 * D ** -0.5),
        "ada_b": nrm(ks[5], (L, 6 * D), 0.02),
        "norm1_g": 1.0 + nrm(ks[6], (L, D), 0.02),
        "norm2_g": 1.0 + nrm(ks[7], (L, D), 0.02),
        "mix_w_in": nrm(ks[8], (N_EVEN, D, MIX_IN), D ** -0.5),
        "pool_w": nrm(ks[9], (N_EVEN, POOL_GROUPS, POOL_GROUP_DIM, POOL_GROUP_DIM), POOL_GROUP_DIM ** -0.5),
        "pool_scale": 1.0 + nrm(ks[10], (N_EVEN, POOL_DIM), 0.1),
        "q_norm_g": 1.0 + nrm(ks[11], (N_EVEN, Q_LORA), 0.02),
        "kv_norm_g": 1.0 + nrm(ks[12], (N_EVEN, KV_LORA), 0.02),
        "w_uq": nrm(ks[13], (N_EVEN, Q_LORA, MLA_HEADS * QK_DIM), Q_LORA ** -0.5),
        "w_ukv": nrm(ks[14], (N_EVEN, KV_LORA, MLA_HEADS * (QK_NOPE + V_DIM)), KV_LORA ** -0.5),
        "mix_w_out": nrm(ks[15], (N_EVEN, MIX_OUT, D), MIX_OUT ** -0.5),
        "conv_w_pw1": nrm(ks[16], (N_ODD, D, 2 * D), D ** -0.5),
        "conv_b_pw1": nrm(ks[17], (N_ODD, 2 * D), 0.02),
        "conv_w_dw": nrm(ks[18], (N_ODD, CONV_WIDTH, D), CONV_WIDTH ** -0.5),
        "conv_b_dw": nrm(ks[19], (N_ODD, D), 0.02),
        "conv_ln_g": 1.0 + nrm(ks[20], (N_ODD, D), 0.02),
        "conv_ln_b": nrm(ks[21], (N_ODD, D), 0.02),
        "conv_w_pw2": nrm(ks[22], (N_ODD, D, D), D ** -0.5),
        "conv_b_pw2": nrm(ks[23], (N_ODD, D), 0.02),
        "router_w": nrm(ks[24], (L, D, N_EXPERTS), D ** -0.5),
        "exp_wg": nrm(ks[25], (L, N_EXPERTS, D, EXPERT_FF), D ** -0.5),
        "exp_wu": nrm(ks[26], (L, N_EXPERTS, D, EXPERT_FF), D ** -0.5),
        "exp_wd": nrm(ks[27], (L, N_EXPERTS, EXPERT_FF, D), EXPERT_FF ** -0.5),
        "final_g": 1.0 + nrm(ks[28], (D,), 0.02),
    }


def reference(x, c, ctx, c_ctx, ada_w, ada_b, norm1_g, norm2_g, mix_w_in, pool_w, pool_scale,
              q_norm_g, kv_norm_g, w_uq, w_ukv, mix_w_out, conv_w_pw1, conv_b_pw1, conv_w_dw,
              conv_b_dw, conv_ln_g, conv_ln_b, conv_w_pw2, conv_b_pw2, router_w, exp_wg, exp_wu,
              exp_wd, final_g):
    n = x.shape[1]
    rope = axial_rope_tables(n)
    last_even = ((DEPTH - 1) // 2) * 2
    for l in range(DEPTH):
        even = (l % 2 == 0)
        ctx_update = l < last_even
        i = l // 2
        m = jax.nn.silu(c) @ ada_w[l] + ada_b[l]
        sh1, sc1, g1, sh2, sc2, g2 = [t[:, None, :] for t in jnp.split(m, 6, axis=-1)]
        if even or ctx_update:
            mc = jax.nn.silu(c_ctx) @ ada_w[l] + ada_b[l]
            csh1, csc1, cg1, csh2, csc2, cg2 = jnp.split(mc, 6, axis=-1)
            hc = modulate(rms_norm(ctx, norm1_g[l]), csh1, csc1)
        h = modulate(rms_norm(x, norm1_g[l]), sh1, sc1)
        if even:
            z = h @ mix_w_in[i]
            q = mla_q(z[..., POOL_DIM:KV_OFF], q_norm_g[i], w_uq[i], rope)
            k, v = mla_kv(z[..., KV_OFF:], kv_norm_g[i], w_ukv[i], rope)
            if ctx_update:
                zc = hc @ mix_w_in[i]
                zc_kv = zc[..., KV_OFF:]
            else:
                zc_kv = hc @ mix_w_in[i][:, KV_OFF:]
            k_c, v_c = mla_kv(zc_kv, kv_norm_g[i], w_ukv[i], None)
            attn = latent_attention(q, k, v, k_c, v_c)
            pool = multiscale_pool(z[..., :POOL_DIM], pool_w[i], pool_scale[i])
            y = jnp.concatenate([pool, attn], axis=-1) @ mix_w_out[i]
            if ctx_update:
                q_c = mla_q(zc[..., POOL_DIM:KV_OFF], q_norm_g[i], w_uq[i], None)
                attn_c = context_attention(q_c, k_c, v_c)
                pool_c = multiscale_pool(zc[..., :POOL_DIM], pool_w[i], pool_scale[i])
                yc = jnp.concatenate([pool_c, attn_c], axis=-1) @ mix_w_out[i]
        else:
            conv_args = (conv_w_pw1[i], conv_b_pw1[i], conv_w_dw[i], conv_b_dw[i],
                         conv_ln_g[i], conv_ln_b[i], conv_w_pw2[i], conv_b_pw2[i])
            y = conformer_conv(h, *conv_args)
            if ctx_update:
                yc = conformer_conv(hc, *conv_args)
        x = x + g1 * y
        h2 = modulate(rms_norm(x, norm2_g[l]), sh2, sc2)
        x = x + g2 * expert_choice_moe(h2, router_w[l], exp_wg[l], exp_wu[l], exp_wd[l])
        if ctx_update:
            ctx = ctx + cg1 * yc
            hc2 = modulate(rms_norm(ctx, norm2_g[l]), csh2, csc2)
            ctx = ctx + cg2 * expert_choice_moe(hc2, router_w[l], exp_wg[l], exp_wu[l], exp_wd[l])
    return rms_norm(x, final_g)
```

```python
import functools
import math

import jax
import jax.numpy as jnp
from jax import lax
from jax.experimental import pallas as pl
from jax.experimental.pallas import tpu as pltpu

F32 = jnp.float32
BF16 = jnp.bfloat16

D_MODEL = 1024
BATCH = 8
SEQ = 4096
CTX_LEN = 256
GRID_W = 64
POOL_WINDOWS = (2, 4, 8, 16)
POOL_GROUP_DIM = 128
POOL_DIM = 512
MLA_HEADS = 8
QK_NOPE = 64
QK_ROPE = 32
V_DIM = 64
Q_LORA = 256
KV_LORA = 128
QK_DIM = QK_NOPE + QK_ROPE
KV_OFF = POOL_DIM + Q_LORA
ROPE_THETA = 10000.0
CONV_WIDTH = 31
N_EXPERTS = 16
CAP = 2 * SEQ // N_EXPERTS
EPS = 1e-6

LANES = 128
HEAD_PAD = 128
POOL_HALO = 8
CONV_HALO = 16
TOKEN_TILE = 512
Q_TILE = 256
MOE_TOKEN_TILE = 1024
VMEM_LIMIT = 52 * 1024 * 1024


def _cparams(sem):
    return pltpu.CompilerParams(dimension_semantics=sem, vmem_limit_bytes=VMEM_LIMIT)


def _rms(xf, g):
    ms = jnp.mean(xf * xf, axis=-1, keepdims=True)
    return xf * lax.rsqrt(ms + EPS) * g


def _dot(a, b):
    return jnp.dot(a, b, preferred_element_type=F32)


def _dot_nt(a, b):
    return lax.dot_general(a, b, (((1,), (1,)), ((), ())), preferred_element_type=F32)


def _ada_kernel(c_ref, w_ref, b_ref, o_ref):
    cv = c_ref[...]
    s = cv * jax.nn.sigmoid(cv)
    o_ref[0] = jnp.dot(s, w_ref[0], precision=lax.Precision.HIGHEST,
                       preferred_element_type=F32) + b_ref[0]


def _ada(cc, ada_w, ada_b):
    L, D, D6 = ada_w.shape
    tn = 1536
    return pl.pallas_call(
        _ada_kernel,
        grid=(L, D6 // tn),
        in_specs=[pl.BlockSpec((16, D), lambda l, j: (0, 0)),
                  pl.BlockSpec((1, D, tn), lambda l, j: (l, 0, j)),
                  pl.BlockSpec((1, 1, tn), lambda l, j: (l, 0, j))],
        out_specs=pl.BlockSpec((1, 16, tn), lambda l, j: (l, 0, j)),
        out_shape=jax.ShapeDtypeStruct((L, 16, D6), F32),
        compiler_params=_cparams(("parallel", "parallel")),
        name="ada",
    )(cc, ada_w, ada_b.reshape(L, 1, D6))


def _mixin_kernel(x_ref, sh_ref, sc_ref, g_ref, wmix_ref, qg_ref, wq_ref, kvg_ref, wkv_ref,
                  cq_ref, sq_ref, ck_ref, sk_ref, q_ref, k_ref, v_ref, zp_ref):
    h = _rms(x_ref[0], g_ref[...]) * (1.0 + sc_ref[0]) + sh_ref[0]
    z = _dot(h.astype(BF16), wmix_ref[...])
    zp_ref[0] = z[:, :POOL_DIM]
    zq = _rms(z[:, POOL_DIM:KV_OFF], qg_ref[...]).astype(BF16)
    qa = _dot(zq, wq_ref[...])
    cq = cq_ref[...]
    sq = sq_ref[...]
    for hh in range(MLA_HEADS):
        qh = qa[:, hh * HEAD_PAD:(hh + 1) * HEAD_PAD]
        q_ref[0, hh] = (qh * cq + pltpu.roll(qh, HEAD_PAD - QK_ROPE, 1) * sq).astype(BF16)
    zkv = _rms(z[:, KV_OFF:KV_OFF + KV_LORA], kvg_ref[...]).astype(BF16)
    kva = _dot(zkv, wkv_ref[...])
    zr = z[:, KV_OFF + KV_LORA:]
    kr = zr * ck_ref[...] + pltpu.roll(zr, HEAD_PAD - QK_ROPE, 1) * sk_ref[...]
    for hh in range(MLA_HEADS):
        k_ref[0, hh] = (kva[:, hh * HEAD_PAD:(hh + 1) * HEAD_PAD] + kr).astype(BF16)
        voff = MLA_HEADS * HEAD_PAD + hh * HEAD_PAD
        v_ref[0, hh] = kva[:, voff:voff + HEAD_PAD].astype(BF16)


def _mixin(x, sh, sc, g, wmix, qg, wq, kvg, wkv, cq, sq, ck, sk, tm):
    B, N, D = x.shape
    H = MLA_HEADS
    vec = lambda: pl.BlockSpec((1, 1, D), lambda b, i: (b, 0, 0))
    full = lambda a: pl.BlockSpec(a.shape, lambda b, i: (0,) * a.ndim)
    tab = lambda: pl.BlockSpec((tm, LANES), lambda b, i: (i, 0))
    hd = lambda: pl.BlockSpec((1, H, tm, HEAD_PAD), lambda b, i: (b, 0, i, 0))
    return pl.pallas_call(
        _mixin_kernel,
        grid=(B, N // tm),
        in_specs=[pl.BlockSpec((1, tm, D), lambda b, i: (b, i, 0)), vec(), vec(), full(g), full(wmix),
                  full(qg), full(wq), full(kvg), full(wkv), tab(), tab(), tab(), tab()],
        out_specs=[hd(), hd(), hd(), pl.BlockSpec((1, tm, POOL_DIM), lambda b, i: (b, i, 0))],
        out_shape=[jax.ShapeDtypeStruct((B, H, N, HEAD_PAD), BF16)] * 3
        + [jax.ShapeDtypeStruct((B, N, POOL_DIM), F32)],
        compiler_params=_cparams(("parallel", "parallel")),
        name="mixin",
    )(x, sh, sc, g, wmix, qg, wq, kvg, wkv, cq, sq, ck, sk)


def _attn_kernel(q_ref, k_ref, v_ref, kc_ref, vc_ref, o_ref):
    acc = None
    for j in range(2):
        q = q_ref[0, j]
        s_c = _dot_nt(q, kc_ref[0, j])
        s = _dot_nt(q, k_ref[0, j])
        m = jnp.maximum(jnp.max(s_c, axis=-1, keepdims=True), jnp.max(s, axis=-1, keepdims=True))
        p_c = jnp.exp2(s_c - m)
        p = jnp.exp2(s - m)
        l = jnp.sum(p_c, axis=-1, keepdims=True) + jnp.sum(p, axis=-1, keepdims=True)
        o = _dot(p_c.astype(BF16), vc_ref[0, j]) + _dot(p.astype(BF16), v_ref[0, j])
        o = o / l
        acc = o if acc is None else acc + o
    o_ref[0] = acc.astype(BF16)


def _attn(q, k, v, kc, vc):
    B, H, N, P = q.shape
    LC = kc.shape[2]
    tq = Q_TILE
    kv = lambda n: pl.BlockSpec((1, 2, n, P), lambda b, p, i: (b, p, 0, 0))
    return pl.pallas_call(
        _attn_kernel,
        grid=(B, H // 2, N // tq),
        in_specs=[pl.BlockSpec((1, 2, tq, P), lambda b, p, i: (b, p, i, 0)),
                  kv(N), kv(N), kv(LC), kv(LC)],
        out_specs=pl.BlockSpec((1, tq, LANES), lambda b, p, i: (b, i, p)),
        out_shape=jax.ShapeDtypeStruct((B, N, MLA_HEADS * V_DIM), BF16),
        compiler_params=_cparams(("parallel", "parallel", "parallel")),
        name="attn",
    )(q, k, v, kc, vc)


def _post(xn, g2n_ref, sh2_ref, sc2_ref, rwa_ref, rwb_ref, xo_ref, h2_ref, aff_ref):
    xo_ref[0] = xn
    h2 = _rms(xn, g2n_ref[...]) * (1.0 + sc2_ref[0]) + sh2_ref[0]
    hi = h2.astype(BF16)
    lo = (h2 - hi.astype(F32)).astype(BF16)
    h2_ref[0] = hi
    la = _dot(hi, rwa_ref[...])
    lb = _dot(lo, rwb_ref[...])
    logits = la[:, :LANES] + la[:, LANES:] + lb
    lane = lax.broadcasted_iota(jnp.int32, logits.shape, 1)
    logits = jnp.where(lane < N_EXPERTS, logits, -1e30)
    m = jnp.max(logits, axis=-1, keepdims=True)
    e = jnp.exp(logits - m)
    aff_ref[0] = e / jnp.sum(e, axis=-1, keepdims=True)


def _post_specs(D, tm):
    vec = lambda: pl.BlockSpec((1, 1, D), lambda b, i: (b, 0, 0))
    in_specs = [pl.BlockSpec((1, D), lambda b, i: (0, 0)), vec(), vec(),
                pl.BlockSpec((D, 2 * LANES), lambda b, i: (0, 0)),
                pl.BlockSpec((D, LANES), lambda b, i: (0, 0))]
    out_specs = [pl.BlockSpec((1, tm, D), lambda b, i: (b, i, 0)),
                 pl.BlockSpec((1, tm, D), lambda b, i: (b, i, 0)),
                 pl.BlockSpec((1, tm, LANES), lambda b, i: (b, i, 0))]
    return in_specs, out_specs


def _post_shapes(B, N, D):
    return [jax.ShapeDtypeStruct((B, N, D), F32), jax.ShapeDtypeStruct((B, N, D), BF16),
            jax.ShapeDtypeStruct((B, N, LANES), F32)]


def _mixout_kernel(x_ref, zp_ref, zprev_ref, znext_ref, attn_ref, pw_ref, ps_ref, wout_ref, g1_ref,
                   g2n_ref, sh2_ref, sc2_ref, rwa_ref, rwb_ref, xo_ref, h2_ref, aff_ref, ext_ref):
    i = pl.program_id(1)
    nt = pl.num_programs(1)
    tm = x_ref.shape[1]
    hl = POOL_HALO
    ext_ref[0:hl] = jnp.where(i > 0, zprev_ref[0], 0.0)
    ext_ref[hl:hl + tm] = zp_ref[0]
    ext_ref[hl + tm:] = jnp.where(i < nt - 1, znext_ref[0], 0.0)
    t = i * tm + lax.broadcasted_iota(jnp.int32, (tm, 1), 0)
    ys = []
    for g, w in enumerate(POOL_WINDOWS):
        cols = slice(g * POOL_GROUP_DIM, (g + 1) * POOL_GROUP_DIM)
        s = None
        for o in range(-(w // 2), w - w // 2):
            v = ext_ref[hl + o:hl + o + tm, cols]
            s = v if s is None else s + v
        lo = jnp.maximum(t - w // 2, 0)
        hi = jnp.minimum(t - w // 2 + w, SEQ)
        d = s / (hi - lo).astype(F32) - ext_ref[hl:hl + tm, cols]
        ys.append(_dot(d.astype(BF16), pw_ref[g]))
    pool = (jnp.concatenate(ys, axis=-1) * ps_ref[...]).astype(BF16)
    y = _dot(pool, wout_ref[0:POOL_DIM]) + _dot(attn_ref[0], wout_ref[POOL_DIM:])
    xn = x_ref[0] + g1_ref[0] * y
    _post(xn, g2n_ref, sh2_ref, sc2_ref, rwa_ref, rwb_ref, xo_ref, h2_ref, aff_ref)


def _mixout(x, zp, attn, pw, ps, wout, g1, g2n, sh2, sc2, rwa, rwb):
    B, N, D = x.shape
    tm = TOKEN_TILE
    hb = tm // POOL_HALO
    nhb = N // POOL_HALO
    vec = lambda: pl.BlockSpec((1, 1, D), lambda b, i: (b, 0, 0))
    pin, pout = _post_specs(D, tm)
    return pl.pallas_call(
        _mixout_kernel,
        grid=(B, N // tm),
        in_specs=[pl.BlockSpec((1, tm, D), lambda b, i: (b, i, 0)),
                  pl.BlockSpec((1, tm, POOL_DIM), lambda b, i: (b, i, 0)),
                  pl.BlockSpec((1, POOL_HALO, POOL_DIM), lambda b, i: (b, jnp.maximum(i * hb - 1, 0), 0)),
                  pl.BlockSpec((1, POOL_HALO, POOL_DIM),
                               lambda b, i: (b, jnp.minimum((i + 1) * hb, nhb - 1), 0)),
                  pl.BlockSpec((1, tm, POOL_DIM), lambda b, i: (b, i, 0)),
                  pl.BlockSpec(pw.shape, lambda b, i: (0, 0, 0)),
                  pl.BlockSpec((1, POOL_DIM), lambda b, i: (0, 0)),
                  pl.BlockSpec(wout.shape, lambda b, i: (0, 0)),
                  vec()] + pin,
        out_specs=pout,
        out_shape=_post_shapes(B, N, D),
        scratch_shapes=[pltpu.VMEM((tm + 2 * POOL_HALO, POOL_DIM), F32)],
        compiler_params=_cparams(("parallel", "parallel")),
        name="mixout",
    )(x, zp, zp, zp, attn, pw, ps, wout, g1, g2n, sh2, sc2, rwa, rwb)


def _topk_kernel(aff_ref, rank_ref, gate_ref):
    n = aff_ref.shape[1]
    nchunk = n // LANES

    def search(it, thr):
        bits = pltpu.bitcast(aff_ref[0], jnp.int32)
        cand = thr | (jnp.int32(1) << (30 - it))
        cnt = jnp.sum(jnp.where(bits >= cand, 1.0, 0.0), axis=0, keepdims=True)
        return jnp.where(cnt >= CAP, cand, thr)

    thr = lax.fori_loop(0, 31, search, jnp.zeros((1, LANES), jnp.int32))

    ri = lax.broadcasted_iota(jnp.int32, (LANES, LANES), 0)
    ci = lax.broadcasted_iota(jnp.int32, (LANES, LANES), 1)
    ltri = jnp.where(ri > ci, 1.0, 0.0).astype(BF16)

    def prefix(mask):
        run = jnp.zeros((1, LANES), F32)
        outs = []
        for c in range(nchunk):
            mc = mask[c * LANES:(c + 1) * LANES]
            outs.append(_dot(ltri, mc.astype(BF16)) + run)
            run = run + jnp.sum(mc, axis=0, keepdims=True)
        return jnp.concatenate(outs, axis=0), run

    a = aff_ref[0]
    bits = pltpu.bitcast(a, jnp.int32)
    gt = jnp.where(bits > thr, 1.0, 0.0)
    eq = jnp.where(bits == thr, 1.0, 0.0)
    need = CAP - jnp.sum(gt, axis=0, keepdims=True)
    pe, _ = prefix(eq)
    sel = gt + eq * jnp.where(pe < need, 1.0, 0.0)
    rank, _ = prefix(sel)
    rank_ref[0] = jnp.where(sel > 0.0, rank, -1.0)
    gate_ref[0] = jnp.where(sel > 0.0, a, 0.0)


def _topk(aff):
    B, N, E = aff.shape
    spec = lambda: pl.BlockSpec((1, N, E), lambda b: (b, 0, 0))
    return pl.pallas_call(
        _topk_kernel,
        grid=(B,),
        in_specs=[spec()],
        out_specs=[spec(), spec()],
        out_shape=[jax.ShapeDtypeStruct((B, N, E), F32)] * 2,
        compiler_params=_cparams(("parallel",)),
        name="topk",
    )(aff)


def _gather_kernel(rank_ref, h2_ref, xs_ref, acc_ref):
    j = pl.program_id(2)
    tn = h2_ref.shape[1]
    slot = lax.broadcasted_iota(jnp.int32, (CAP, tn), 0).astype(F32)
    p = jnp.where(slot == rank_ref[0, 0], 1.0, 0.0).astype(BF16)
    part = _dot(p, h2_ref[0])

    @pl.when(j == 0)
    def _():
        acc_ref[...] = part

    @pl.when(j > 0)
    def _():
        acc_ref[...] += part

    @pl.when(j == pl.num_programs(2) - 1)
    def _():
        xs_ref[0, 0] = acc_ref[...].astype(BF16)


def _gather(rank_t, h2):
    B, N, D = h2.shape
    E = N_EXPERTS
    tn = MOE_TOKEN_TILE
    return pl.pallas_call(
        _gather_kernel,
        grid=(B, E, N // tn),
        in_specs=[pl.BlockSpec((1, 1, 1, tn), lambda b, e, j: (b, e, 0, j)),
                  pl.BlockSpec((1, tn, D), lambda b, e, j: (b, j, 0))],
        out_specs=pl.BlockSpec((1, 1, CAP, D), lambda b, e, j: (b, e, 0, 0)),
        out_shape=jax.ShapeDtypeStruct((B, E, CAP, D), BF16),
        scratch_shapes=[pltpu.VMEM((CAP, D), F32)],
        compiler_params=_cparams(("parallel", "parallel", "arbitrary")),
        name="gather",
    )(rank_t, h2)


def _ffn_kernel(xs_ref, wg_ref, wu_ref, wd_ref, y_ref, wgb_ref, wub_ref, wdb_ref):
    @pl.when(pl.program_id(1) == 0)
    def _():
        wgb_ref[...] = wg_ref[0].astype(BF16)
        wub_ref[...] = wu_ref[0].astype(BF16)
        wdb_ref[...] = wd_ref[0].astype(BF16)

    xs = xs_ref[0, 0]
    a = _dot(xs, wgb_ref[...])
    u = _dot(xs, wub_ref[...])
    hm = (a * jax.nn.sigmoid(a) * u).astype(BF16)
    y_ref[0, 0] = _dot(hm, wdb_ref[...]).astype(BF16)


def _ffn(xs, wg, wu, wd):
    B, E, C, D = xs.shape
    F = wg.shape[2]
    return pl.pallas_call(
        _ffn_kernel,
        grid=(E, B),
        in_specs=[pl.BlockSpec((1, 1, C, D), lambda e, b: (b, e, 0, 0)),
                  pl.BlockSpec((1, D, F), lambda e, b: (e, 0, 0)),
                  pl.BlockSpec((1, D, F), lambda e, b: (e, 0, 0)),
                  pl.BlockSpec((1, F, D), lambda e, b: (e, 0, 0))],
        out_specs=pl.BlockSpec((1, 1, C, D), lambda e, b: (b, e, 0, 0)),
        out_shape=jax.ShapeDtypeStruct((B, E, C, D), BF16),
        scratch_shapes=[pltpu.VMEM((D, F), BF16), pltpu.VMEM((D, F), BF16), pltpu.VMEM((F, D), BF16)],
        compiler_params=_cparams(("arbitrary", "arbitrary")),
        name="ffn",
    )(xs, wg, wu, wd)


def _combine_kernel(y_ref, rank_ref, gate_ref, x_ref, g2_ref, fg_ref, o_ref, acc_ref, *, final):
    e = pl.program_id(2)
    tn = x_ref.shape[1]
    lane = lax.broadcasted_iota(jnp.int32, (tn, LANES), 1)
    rk = jnp.sum(jnp.where(lane == e, rank_ref[0], 0.0), axis=-1, keepdims=True)
    gt = jnp.sum(jnp.where(lane == e, gate_ref[0], 0.0), axis=-1, keepdims=True)
    slot = lax.broadcasted_iota(jnp.int32, (tn, CAP), 1).astype(F32)
    pt = jnp.where(slot == rk, 1.0, 0.0).astype(BF16)
    part = _dot(pt, y_ref[0, 0]) * gt

    @pl.when(e == 0)
    def _():
        acc_ref[...] = part

    @pl.when(e > 0)
    def _():
        acc_ref[...] += part

    @pl.when(e == pl.num_programs(2) - 1)
    def _():
        xn = x_ref[0] + g2_ref[0] * acc_ref[...]
        if final:
            xn = _rms(xn, fg_ref[...])
        o_ref[0] = xn


def _combine(y, rank, gate, x, g2, fg, final):
    B, N, D = x.shape
    E = N_EXPERTS
    tn = MOE_TOKEN_TILE
    return pl.pallas_call(
        functools.partial(_combine_kernel, final=final),
        grid=(B, N // tn, E),
        in_specs=[pl.BlockSpec((1, 1, CAP, D), lambda b, j, e: (b, e, 0, 0)),
                  pl.BlockSpec((1, tn, LANES), lambda b, j, e: (b, j, 0)),
                  pl.BlockSpec((1, tn, LANES), lambda b, j, e: (b, j, 0)),
                  pl.BlockSpec((1, tn, D), lambda b, j, e: (b, j, 0)),
                  pl.BlockSpec((1, 1, D), lambda b, j, e: (b, 0, 0)),
                  pl.BlockSpec((1, D), lambda b, j, e: (0, 0))],
        out_specs=pl.BlockSpec((1, tn, D), lambda b, j, e: (b, j, 0)),
        out_shape=jax.ShapeDtypeStruct((B, N, D), F32),
        scratch_shapes=[pltpu.VMEM((tn, D), F32)],
        compiler_params=_cparams(("parallel", "parallel", "arbitrary")),
        name="combine",
    )(y, rank, gate, x, g2, fg)


def _moe(x, h2, aff, g2, wg, wu, wd, fg, final):
    rank, gate = _topk(aff)
    B, N, _ = rank.shape
    rank_t = jnp.transpose(rank[:, :, :N_EXPERTS], (0, 2, 1)).reshape(B, N_EXPERTS, 1, N)
    xs = _gather(rank_t, h2)
    y = _ffn(xs, wg, wu, wd)
    return _combine(y, rank, gate, x, g2, fg, final)


def _pw1_kernel(x_ref, sh_ref, sc_ref, g_ref, w_ref, b_ref, u_ref):
    d = x_ref.shape[2]
    h = _rms(x_ref[0], g_ref[...]) * (1.0 + sc_ref[0]) + sh_ref[0]
    z = _dot(h.astype(BF16), w_ref[...]) + b_ref[...]
    u_ref[0] = z[:, :d] * jax.nn.sigmoid(z[:, d:])


def _pw1(x, sh, sc, g, w, b):
    B, N, D = x.shape
    tm = TOKEN_TILE
    vec = lambda: pl.BlockSpec((1, 1, D), lambda b, i: (b, 0, 0))
    return pl.pallas_call(
        _pw1_kernel,
        grid=(B, N // tm),
        in_specs=[pl.BlockSpec((1, tm, D), lambda b, i: (b, i, 0)), vec(), vec(),
                  pl.BlockSpec((1, D), lambda b, i: (0, 0)),
                  pl.BlockSpec((D, 2 * D), lambda b, i: (0, 0)),
                  pl.BlockSpec((1, 2 * D), lambda b, i: (0, 0))],
        out_specs=pl.BlockSpec((1, tm, D), lambda b, i: (b, i, 0)),
        out_shape=jax.ShapeDtypeStruct((B, N, D), F32),
        compiler_params=_cparams(("parallel", "parallel")),
        name="pw1",
    )(x, sh, sc, g, w, b)


def _conv_kernel(x_ref, u_ref, uprev_ref, unext_ref, wdw_ref, bdw_ref, lng_ref, lnb_ref, w2_ref, b2_ref,
                 g1_ref, g2n_ref, sh2_ref, sc2_ref, rwa_ref, rwb_ref, xo_ref, h2_ref, aff_ref, ext_ref):
    i = pl.program_id(1)
    nt = pl.num_programs(1)
    tm = x_ref.shape[1]
    hl = CONV_HALO
    ext_ref[0:hl] = jnp.where(i > 0, uprev_ref[0], 0.0)
    ext_ref[hl:hl + tm] = u_ref[0]
    ext_ref[hl + tm:] = jnp.where(i < nt - 1, unext_ref[0], 0.0)
    acc = None
    for k in range(CONV_WIDTH):
        off = hl + k - CONV_WIDTH // 2
        v = ext_ref[off:off + tm, :] * wdw_ref[k:k + 1, :]
        acc = v if acc is None else acc + v
    acc = acc + bdw_ref[...]
    mu = jnp.mean(acc, axis=-1, keepdims=True)
    cen = acc - mu
    var = jnp.mean(cen * cen, axis=-1, keepdims=True)
    yn = cen * lax.rsqrt(var + EPS) * lng_ref[...] + lnb_ref[...]
    yn = yn * jax.nn.sigmoid(yn)
    y = _dot(yn.astype(BF16), w2_ref[...]) + b2_ref[...]
    xn = x_ref[0] + g1_ref[0] * y
    _post(xn, g2n_ref, sh2_ref, sc2_ref, rwa_ref, rwb_ref, xo_ref, h2_ref, aff_ref)


def _conv(x, u, wdw, bdw, lng, lnb, w2, b2, g1, g2n, sh2, sc2, rwa, rwb):
    B, N, D = x.shape
    tm = TOKEN_TILE
    hb = tm // CONV_HALO
    nhb = N // CONV_HALO
    vec = lambda: pl.BlockSpec((1, 1, D), lambda b, i: (b, 0, 0))
    row = lambda: pl.BlockSpec((1, D), lambda b, i: (0, 0))
    pin, pout = _post_specs(D, tm)
    return pl.pallas_call(
        _conv_kernel,
        grid=(B, N // tm),
        in_specs=[pl.BlockSpec((1, tm, D), lambda b, i: (b, i, 0)),
                  pl.BlockSpec((1, tm, D), lambda b, i: (b, i, 0)),
                  pl.BlockSpec((1, CONV_HALO, D), lambda b, i: (b, jnp.maximum(i * hb - 1, 0), 0)),
                  pl.BlockSpec((1, CONV_HALO, D), lambda b, i: (b, jnp.minimum((i + 1) * hb, nhb - 1), 0)),
                  pl.BlockSpec(wdw.shape, lambda b, i: (0, 0)),
                  row(), row(), row(),
                  pl.BlockSpec((D, D), lambda b, i: (0, 0)),
                  row(), vec()] + pin,
        out_specs=pout,
        out_shape=_post_shapes(B, N, D),
        scratch_shapes=[pltpu.VMEM((tm + 2 * CONV_HALO, D), F32)],
        compiler_params=_cparams(("parallel", "parallel")),
        name="conv",
    )(x, u, u, u, wdw, bdw, lng, lnb, w2, b2, g1, g2n, sh2, sc2, rwa, rwb)


def _rope_tables(n):
    rows = n // GRID_W
    row = jnp.repeat(jnp.arange(rows), GRID_W).astype(F32)
    col = jnp.tile(jnp.arange(GRID_W), rows).astype(F32)
    per_axis = QK_ROPE // 2
    inv_freq = 1.0 / (ROPE_THETA ** (jnp.arange(0, per_axis, 2, dtype=F32) / per_axis))
    ang = jnp.stack([row[:, None] * inv_freq, col[:, None] * inv_freq], axis=1)
    cos, sin = jnp.cos(ang), jnp.sin(ang)
    cos32 = jnp.broadcast_to(cos[:, :, None, :], (n, 2, 2, QK_ROPE // 4)).reshape(n, QK_ROPE)
    sin32 = jnp.stack([-sin, sin], axis=2).reshape(n, QK_ROPE)
    return cos32, sin32


def _head_tables(cos32, sin32, lead, scale):
    n = cos32.shape[0]
    cosf = jnp.concatenate([jnp.full((n, QK_NOPE), lead, F32), cos32, jnp.zeros((n, QK_ROPE), F32)], axis=1)
    sinf = jnp.concatenate([jnp.zeros((n, QK_NOPE), F32), sin32, jnp.zeros((n, QK_ROPE), F32)], axis=1)
    return cosf * scale, sinf * scale


def _layer0_weights(mix_w_in, w_uq, w_ukv):
    D = mix_w_in.shape[0]
    perm = jnp.arange(QK_ROPE) ^ (QK_ROPE // 4)
    rope_cols = mix_w_in[:, KV_OFF + KV_LORA:]
    wmix = jnp.concatenate([mix_w_in[:, :KV_OFF + KV_LORA], jnp.zeros((D, QK_NOPE), F32),
                            rope_cols, rope_cols[:, perm]], axis=1).astype(BF16)
    wq3 = w_uq.reshape(Q_LORA, MLA_HEADS, QK_DIM)
    wq = jnp.concatenate([wq3, wq3[:, :, QK_NOPE:][:, :, perm]], axis=2).reshape(Q_LORA, MLA_HEADS * HEAD_PAD)
    wkv3 = w_ukv.reshape(KV_LORA, MLA_HEADS, QK_NOPE + V_DIM)
    zpad = jnp.zeros((KV_LORA, MLA_HEADS, HEAD_PAD - QK_NOPE), F32)
    wk = jnp.concatenate([wkv3[:, :, :QK_NOPE], zpad], axis=2)
    wv_even = jnp.concatenate([wkv3[:, :, QK_NOPE:], zpad], axis=2)
    wv_odd = jnp.concatenate([zpad, wkv3[:, :, QK_NOPE:]], axis=2)
    odd = (jnp.arange(MLA_HEADS) % 2 == 1)[None, :, None]
    wv = jnp.where(odd, wv_odd, wv_even)
    wkv = jnp.concatenate([wk.reshape(KV_LORA, -1), wv.reshape(KV_LORA, -1)], axis=1)
    return wmix, wq.astype(BF16), wkv.astype(BF16)


def _router_weights(rw):
    D, E = rw.shape
    rwp = jnp.concatenate([rw, jnp.zeros((D, LANES - E), F32)], axis=1)
    hi = rwp.astype(BF16)
    lo = (rwp - hi.astype(F32)).astype(BF16)
    return jnp.concatenate([hi, lo], axis=1), hi


def kernel(x, c, ctx, c_ctx, ada_w, ada_b, norm1_g, norm2_g, mix_w_in, pool_w, pool_scale, q_norm_g, kv_norm_g, w_uq, w_ukv, mix_w_out, conv_w_pw1, conv_b_pw1, conv_w_dw, conv_b_dw, conv_ln_g, conv_ln_b, conv_w_pw2, conv_b_pw2, router_w, exp_wg, exp_wu, exp_wd, final_g):
    B, N, D = x.shape
    LC = ctx.shape[1]

    cc = jnp.concatenate([c, c_ctx[None, :], jnp.zeros((16 - B - 1, D), F32)], axis=0)
    mods = _ada(cc, ada_w, ada_b)

    def mod(l, k):
        return mods[l, :B, k * D:(k + 1) * D].reshape(B, 1, D)

    def mod_ctx(l, k):
        return jnp.broadcast_to(mods[l, B, k * D:(k + 1) * D].reshape(1, 1, D), (B, 1, D))

    row = lambda v: v.reshape(1, -1)

    wmix, wq, wkv = _layer0_weights(mix_w_in[0], w_uq[0], w_ukv[0])
    qscale = (QK_DIM ** -0.5) * math.log2(math.e)
    cos32, sin32 = _rope_tables(N)
    cq, sq = _head_tables(cos32, sin32, 1.0, qscale)
    ck, sk = _head_tables(cos32, sin32, 0.0, 1.0)
    one32, zero32 = jnp.ones((LC, QK_ROPE), F32), jnp.zeros((LC, QK_ROPE), F32)
    cqc, sqc = _head_tables(one32, zero32, 1.0, qscale)
    ckc, skc = _head_tables(one32, zero32, 0.0, 1.0)
    lw = (row(norm1_g[0]), wmix, row(q_norm_g[0]), wq, row(kv_norm_g[0]), wkv)
    q, k, v, zp = _mixin(x, mod(0, 0), mod(0, 1), *lw, cq, sq, ck, sk, TOKEN_TILE)
    _, kc, vc, _ = _mixin(ctx, mod_ctx(0, 0), mod_ctx(0, 1), *lw, cqc, sqc, ckc, skc, LC)
    attn = _attn(q, k, v, kc, vc)
    rwa, rwb = _router_weights(router_w[0])
    x, h2, aff = _mixout(x, zp, attn, pool_w[0].astype(BF16), row(pool_scale[0]), mix_w_out[0].astype(BF16),
                         mod(0, 2), row(norm2_g[0]), mod(0, 3), mod(0, 4), rwa, rwb)
    x = _moe(x, h2, aff, mod(0, 5), exp_wg[0], exp_wu[0], exp_wd[0], row(final_g), False)

    u = _pw1(x, mod(1, 0), mod(1, 1), row(norm1_g[1]), conv_w_pw1[0].astype(BF16), row(conv_b_pw1[0]))
    wdw = jnp.concatenate([conv_w_dw[0], jnp.zeros((1, D), F32)], axis=0)
    rwa, rwb = _router_weights(router_w[1])
    x, h2, aff = _conv(x, u, wdw, row(conv_b_dw[0]), row(conv_ln_g[0]), row(conv_ln_b[0]),
                       conv_w_pw2[0].astype(BF16), row(conv_b_pw2[0]),
                       mod(1, 2), row(norm2_g[1]), mod(1, 3), mod(1, 4), rwa, rwb)
    return _moe(x, h2, aff, mod(1, 5), exp_wg[1], exp_wu[1], exp_wd[1], row(final_g), True)
```

```python
import functools
import math

import jax
import jax.numpy as jnp
from jax import lax
from jax.experimental import pallas as pl
from jax.experimental.pallas import tpu as pltpu

F32 = jnp.float32
BF16 = jnp.bfloat16

D_MODEL = 1024
BATCH = 8
SEQ = 4096
CTX_LEN = 256
GRID_W = 64
POOL_WINDOWS = (2, 4, 8, 16)
POOL_GROUP_DIM = 128
POOL_DIM = 512
MLA_HEADS = 8
QK_NOPE = 64
QK_ROPE = 32
V_DIM = 64
Q_LORA = 256
KV_LORA = 128
QK_DIM = QK_NOPE + QK_ROPE
KV_OFF = POOL_DIM + Q_LORA
ROPE_THETA = 10000.0
CONV_WIDTH = 31
N_EXPERTS = 16
CAP = 2 * SEQ // N_EXPERTS
EPS = 1e-6

LANES = 128
HEAD_PAD = 128
POOL_HALO = 8
CONV_HALO = 16
TOKEN_TILE = 512
Q_TILE = 256
MOE_BLOCK = 256
SLOT_WIN = 64
SLOT_ALIGN = 16
WIN_OVERFLOW = 4096.0
VMEM_LIMIT = 52 * 1024 * 1024


def _cparams(sem):
    return pltpu.CompilerParams(dimension_semantics=sem, vmem_limit_bytes=VMEM_LIMIT)


def _rms(xf, g):
    ms = jnp.mean(xf * xf, axis=-1, keepdims=True)
    return xf * lax.rsqrt(ms + EPS) * g


def _dot(a, b):
    return jnp.dot(a, b, preferred_element_type=F32)


def _dot_nt(a, b):
    return lax.dot_general(a, b, (((1,), (1,)), ((), ())), preferred_element_type=F32)


def _ada_kernel(c_ref, w_ref, b_ref, o_ref):
    cv = c_ref[...]
    s = cv * jax.nn.sigmoid(cv)
    o_ref[0] = jnp.dot(s, w_ref[0], precision=lax.Precision.HIGHEST,
                       preferred_element_type=F32) + b_ref[0]


def _ada(cc, ada_w, ada_b):
    L, D, D6 = ada_w.shape
    tn = 1536
    return pl.pallas_call(
        _ada_kernel,
        grid=(L, D6 // tn),
        in_specs=[pl.BlockSpec((16, D), lambda l, j: (0, 0)),
                  pl.BlockSpec((1, D, tn), lambda l, j: (l, 0, j)),
                  pl.BlockSpec((1, 1, tn), lambda l, j: (l, 0, j))],
        out_specs=pl.BlockSpec((1, 16, tn), lambda l, j: (l, 0, j)),
        out_shape=jax.ShapeDtypeStruct((L, 16, D6), F32),
        compiler_params=_cparams(("parallel", "parallel")),
        name="ada",
    )(cc, ada_w, ada_b.reshape(L, 1, D6))


def _mixin_kernel(x_ref, sh_ref, sc_ref, g_ref, wmix_ref, qg_ref, wq_ref, kvg_ref, wkv_ref,
                  cq_ref, sq_ref, ck_ref, sk_ref, q_ref, k_ref, v_ref, zp_ref):
    h = _rms(x_ref[0], g_ref[...]) * (1.0 + sc_ref[0]) + sh_ref[0]
    z = _dot(h.astype(BF16), wmix_ref[...])
    zp_ref[0] = z[:, :POOL_DIM]
    zq = _rms(z[:, POOL_DIM:KV_OFF], qg_ref[...]).astype(BF16)
    qa = _dot(zq, wq_ref[...])
    cq = cq_ref[...]
    sq = sq_ref[...]
    for hh in range(MLA_HEADS):
        qh = qa[:, hh * HEAD_PAD:(hh + 1) * HEAD_PAD]
        q_ref[0, hh] = (qh * cq + pltpu.roll(qh, HEAD_PAD - QK_ROPE, 1) * sq).astype(BF16)
    zkv = _rms(z[:, KV_OFF:KV_OFF + KV_LORA], kvg_ref[...]).astype(BF16)
    kva = _dot(zkv, wkv_ref[...])
    zr = z[:, KV_OFF + KV_LORA:]
    kr = zr * ck_ref[...] + pltpu.roll(zr, HEAD_PAD - QK_ROPE, 1) * sk_ref[...]
    for hh in range(MLA_HEADS):
        k_ref[0, hh] = (kva[:, hh * HEAD_PAD:(hh + 1) * HEAD_PAD] + kr).astype(BF16)
        voff = MLA_HEADS * HEAD_PAD + hh * HEAD_PAD
        v_ref[0, hh] = kva[:, voff:voff + HEAD_PAD].astype(BF16)


def _mixin(x, sh, sc, g, wmix, qg, wq, kvg, wkv, cq, sq, ck, sk, tm):
    B, N, D = x.shape
    H = MLA_HEADS
    vec = lambda: pl.BlockSpec((1, 1, D), lambda b, i: (b, 0, 0))
    full = lambda a: pl.BlockSpec(a.shape, lambda b, i: (0,) * a.ndim)
    tab = lambda: pl.BlockSpec((tm, LANES), lambda b, i: (i, 0))
    hd = lambda: pl.BlockSpec((1, H, tm, HEAD_PAD), lambda b, i: (b, 0, i, 0))
    return pl.pallas_call(
        _mixin_kernel,
        grid=(B, N // tm),
        in_specs=[pl.BlockSpec((1, tm, D), lambda b, i: (b, i, 0)), vec(), vec(), full(g), full(wmix),
                  full(qg), full(wq), full(kvg), full(wkv), tab(), tab(), tab(), tab()],
        out_specs=[hd(), hd(), hd(), pl.BlockSpec((1, tm, POOL_DIM), lambda b, i: (b, i, 0))],
        out_shape=[jax.ShapeDtypeStruct((B, H, N, HEAD_PAD), BF16)] * 3
        + [jax.ShapeDtypeStruct((B, N, POOL_DIM), F32)],
        compiler_params=_cparams(("parallel", "parallel")),
        name="mixin",
    )(x, sh, sc, g, wmix, qg, wq, kvg, wkv, cq, sq, ck, sk)


def _attn_kernel(q_ref, k_ref, v_ref, kc_ref, vc_ref, o_ref):
    acc = None
    for j in range(2):
        q = q_ref[0, j]
        s_c = _dot_nt(q, kc_ref[0, j])
        s = _dot_nt(q, k_ref[0, j])
        m = jnp.maximum(jnp.max(s_c, axis=-1, keepdims=True), jnp.max(s, axis=-1, keepdims=True))
        p_c = jnp.exp2(s_c - m)
        p = jnp.exp2(s - m)
        l = jnp.sum(p_c, axis=-1, keepdims=True) + jnp.sum(p, axis=-1, keepdims=True)
        o = _dot(p_c.astype(BF16), vc_ref[0, j]) + _dot(p.astype(BF16), v_ref[0, j])
        o = o / l
        acc = o if acc is None else acc + o
    o_ref[0] = acc.astype(BF16)


def _attn(q, k, v, kc, vc):
    B, H, N, P = q.shape
    LC = kc.shape[2]
    tq = Q_TILE
    kv = lambda n: pl.BlockSpec((1, 2, n, P), lambda b, p, i: (b, p, 0, 0))
    return pl.pallas_call(
        _attn_kernel,
        grid=(B, H // 2, N // tq),
        in_specs=[pl.BlockSpec((1, 2, tq, P), lambda b, p, i: (b, p, i, 0)),
                  kv(N), kv(N), kv(LC), kv(LC)],
        out_specs=pl.BlockSpec((1, tq, LANES), lambda b, p, i: (b, i, p)),
        out_shape=jax.ShapeDtypeStruct((B, N, MLA_HEADS * V_DIM), BF16),
        compiler_params=_cparams(("parallel", "parallel", "parallel")),
        name="attn",
    )(q, k, v, kc, vc)


def _post(xn, g2n_ref, sh2_ref, sc2_ref, rwa_ref, rwb_ref, xo_ref, h2_ref, aff_ref):
    xo_ref[0] = xn
    h2 = _rms(xn, g2n_ref[...]) * (1.0 + sc2_ref[0]) + sh2_ref[0]
    hi = h2.astype(BF16)
    lo = (h2 - hi.astype(F32)).astype(BF16)
    h2_ref[0] = hi
    la = _dot(hi, rwa_ref[...])
    lb = _dot(lo, rwb_ref[...])
    logits = la[:, :LANES] + la[:, LANES:] + lb
    lane = lax.broadcasted_iota(jnp.int32, logits.shape, 1)
    logits = jnp.where(lane < N_EXPERTS, logits, -1e30)
    m = jnp.max(logits, axis=-1, keepdims=True)
    e = jnp.exp(logits - m)
    aff_ref[0] = e / jnp.sum(e, axis=-1, keepdims=True)


def _post_specs(D, tm):
    vec = lambda: pl.BlockSpec((1, 1, D), lambda b, i: (b, 0, 0))
    in_specs = [pl.BlockSpec((1, D), lambda b, i: (0, 0)), vec(), vec(),
                pl.BlockSpec((D, 2 * LANES), lambda b, i: (0, 0)),
                pl.BlockSpec((D, LANES), lambda b, i: (0, 0))]
    out_specs = [pl.BlockSpec((1, tm, D), lambda b, i: (b, i, 0)),
                 pl.BlockSpec((1, tm, D), lambda b, i: (b, i, 0)),
                 pl.BlockSpec((1, tm, LANES), lambda b, i: (b, i, 0))]
    return in_specs, out_specs


def _post_shapes(B, N, D):
    return [jax.ShapeDtypeStruct((B, N, D), F32), jax.ShapeDtypeStruct((B, N, D), BF16),
            jax.ShapeDtypeStruct((B, N, LANES), F32)]


def _mixout_kernel(x_ref, zp_ref, zprev_ref, znext_ref, attn_ref, pw_ref, ps_ref, wout_ref, g1_ref,
                   g2n_ref, sh2_ref, sc2_ref, rwa_ref, rwb_ref, xo_ref, h2_ref, aff_ref, ext_ref):
    i = pl.program_id(1)
    nt = pl.num_programs(1)
    tm = x_ref.shape[1]
    hl = POOL_HALO
    ext_ref[0:hl] = jnp.where(i > 0, zprev_ref[0], 0.0)
    ext_ref[hl:hl + tm] = zp_ref[0]
    ext_ref[hl + tm:] = jnp.where(i < nt - 1, znext_ref[0], 0.0)
    t = i * tm + lax.broadcasted_iota(jnp.int32, (tm, 1), 0)
    ys = []
    for g, w in enumerate(POOL_WINDOWS):
        cols = slice(g * POOL_GROUP_DIM, (g + 1) * POOL_GROUP_DIM)
        s = None
        for o in range(-(w // 2), w - w // 2):
            v = ext_ref[hl + o:hl + o + tm, cols]
            s = v if s is None else s + v
        lo = jnp.maximum(t - w // 2, 0)
        hi = jnp.minimum(t - w // 2 + w, SEQ)
        d = s / (hi - lo).astype(F32) - ext_ref[hl:hl + tm, cols]
        ys.append(_dot(d.astype(BF16), pw_ref[g]))
    pool = (jnp.concatenate(ys, axis=-1) * ps_ref[...]).astype(BF16)
    y = _dot(pool, wout_ref[0:POOL_DIM]) + _dot(attn_ref[0], wout_ref[POOL_DIM:])
    xn = x_ref[0] + g1_ref[0] * y
    _post(xn, g2n_ref, sh2_ref, sc2_ref, rwa_ref, rwb_ref, xo_ref, h2_ref, aff_ref)


def _mixout(x, zp, attn, pw, ps, wout, g1, g2n, sh2, sc2, rwa, rwb):
    B, N, D = x.shape
    tm = TOKEN_TILE
    hb = tm // POOL_HALO
    nhb = N // POOL_HALO
    vec = lambda: pl.BlockSpec((1, 1, D), lambda b, i: (b, 0, 0))
    pin, pout = _post_specs(D, tm)
    return pl.pallas_call(
        _mixout_kernel,
        grid=(B, N // tm),
        in_specs=[pl.BlockSpec((1, tm, D), lambda b, i: (b, i, 0)),
                  pl.BlockSpec((1, tm, POOL_DIM), lambda b, i: (b, i, 0)),
                  pl.BlockSpec((1, POOL_HALO, POOL_DIM), lambda b, i: (b, jnp.maximum(i * hb - 1, 0), 0)),
                  pl.BlockSpec((1, POOL_HALO, POOL_DIM),
                               lambda b, i: (b, jnp.minimum((i + 1) * hb, nhb - 1), 0)),
                  pl.BlockSpec((1, tm, POOL_DIM), lambda b, i: (b, i, 0)),
                  pl.BlockSpec(pw.shape, lambda b, i: (0, 0, 0)),
                  pl.BlockSpec((1, POOL_DIM), lambda b, i: (0, 0)),
                  pl.BlockSpec(wout.shape, lambda b, i: (0, 0)),
                  vec()] + pin,
        out_specs=pout,
        out_shape=_post_shapes(B, N, D),
        scratch_shapes=[pltpu.VMEM((tm + 2 * POOL_HALO, POOL_DIM), F32)],
        compiler_params=_cparams(("parallel", "parallel")),
        name="mixout",
    )(x, zp, zp, zp, attn, pw, ps, wout, g1, g2n, sh2, sc2, rwa, rwb)


def _topk_kernel(aff_ref, rank_ref, rank_t_ref, gp_ref, wa_ref, wc_ref):
    n = aff_ref.shape[1]
    nchunk = n // LANES

    def search(it, thr):
        bits = pltpu.bitcast(aff_ref[0], jnp.int32)
        cand = thr | (jnp.int32(1) << (30 - it))
        cnt = jnp.sum(jnp.where(bits >= cand, 1.0, 0.0), axis=0, keepdims=True)
        return jnp.where(cnt >= CAP, cand, thr)

    thr = lax.fori_loop(0, 31, search, jnp.zeros((1, LANES), jnp.int32))

    ri = lax.broadcasted_iota(jnp.int32, (LANES, LANES), 0)
    ci = lax.broadcasted_iota(jnp.int32, (LANES, LANES), 1)
    ltri = jnp.where(ri > ci, 1.0, 0.0).astype(BF16)

    def prefix(mask):
        run = jnp.zeros((1, LANES), F32)
        outs = []
        for c in range(nchunk):
            mc = mask[c * LANES:(c + 1) * LANES]
            outs.append(_dot(ltri, mc.astype(BF16)) + run)
            run = run + jnp.sum(mc, axis=0, keepdims=True)
        return jnp.concatenate(outs, axis=0), run

    a = aff_ref[0]
    bits = pltpu.bitcast(a, jnp.int32)
    gt = jnp.where(bits > thr, 1.0, 0.0)
    eq = jnp.where(bits == thr, 1.0, 0.0)
    need = CAP - jnp.sum(gt, axis=0, keepdims=True)
    pe, _ = prefix(eq)
    sel = gt + eq * jnp.where(pe < need, 1.0, 0.0)
    pref, total = prefix(sel)
    rank = jnp.where(sel > 0.0, pref, -1.0)
    rank_ref[0] = rank
    for c in range(nchunk):
        rank_t_ref[0, :, c * LANES:(c + 1) * LANES] = rank[c * LANES:(c + 1) * LANES].T[:N_EXPERTS]
    gate = jnp.where(sel > 0.0, a, 0.0)
    g_hi = gate.astype(BF16)
    r1 = gate - g_hi.astype(F32)
    g_mid = r1.astype(BF16)
    g_lo = (r1 - g_mid.astype(F32)).astype(BF16)
    gp_ref[0] = jnp.concatenate([g_hi, g_mid, g_lo], axis=-1)
    lo =[pref[j * MOE_BLOCK:j * MOE_BLOCK + 1] for j in range(n // MOE_BLOCK)] + [total]
    wa, wc = [], []
    for j in range(n // MOE_BLOCK):
        start = jnp.minimum(jnp.floor(lo[j] * (1.0 / SLOT_ALIGN)) * SLOT_ALIGN, float(CAP - SLOT_WIN))
        wa.append(start)
        wc.append(start + jnp.where(lo[j + 1] > start + SLOT_WIN, WIN_OVERFLOW, 0.0))
    wa_ref[0] = jnp.concatenate(wa, axis=0)
    wc_ref[0] = jnp.concatenate(wc, axis=0)


def _topk(aff):
    B, N, E = aff.shape
    nj = N // MOE_BLOCK
    spec = lambda: pl.BlockSpec((1, N, E), lambda b: (b, 0, 0))
    wspec = lambda: pl.BlockSpec((1, nj, E), lambda b: (b, 0, 0))
    return pl.pallas_call(
        _topk_kernel,
        grid=(B,),
        in_specs=[spec()],
        out_specs=[spec(), pl.BlockSpec((1, N_EXPERTS, N), lambda b: (b, 0, 0)),
                   pl.BlockSpec((1, N, 3 * E), lambda b: (b, 0, 0)), wspec(), wspec()],
        out_shape=[jax.ShapeDtypeStruct((B, N, E), F32), jax.ShapeDtypeStruct((B, N_EXPERTS, N), F32),
                   jax.ShapeDtypeStruct((B, N, 3 * E), BF16),
                   jax.ShapeDtypeStruct((B, nj, E), F32), jax.ShapeDtypeStruct((B, nj, E), F32)],
        compiler_params=_cparams(("parallel",)),
        name="topk",
    )(aff)


def _gather_kernel(wa_s, ovf_s, rank_t_ref, h2_ref, gp_ref, xs_ref, gs_ref):
    b = pl.program_id(0)
    j = pl.program_id(1)
    base = (b * pl.num_programs(1) + j) * N_EXPERTS
    tb = h2_ref.shape[1]

    @pl.when(j == 0)
    def _():
        xs_ref[...] = jnp.zeros_like(xs_ref)
        gs_ref[...] = jnp.zeros_like(gs_ref)

    def gate_sum(g):
        return g[:, :LANES] + g[:, LANES:2 * LANES] + g[:, 2 * LANES:]

    sub = lax.broadcasted_iota(jnp.int32, (SLOT_WIN, tb), 0).astype(F32)
    ps = []
    for e in range(N_EXPERTS):
        wcmp = (wa_s[base + e] + ovf_s[base + e] * int(WIN_OVERFLOW)).astype(F32)
        ps.append(jnp.where(rank_t_ref[0, e:e + 1, :] - wcmp == sub, 1.0, 0.0).astype(BF16))
    pcat = jnp.concatenate(ps, axis=0)
    res = _dot(pcat, h2_ref[0])
    gres = gate_sum(_dot(pcat, gp_ref[0]))
    for e in range(N_EXPERTS):
        wa = pl.multiple_of(wa_s[base + e], SLOT_ALIGN)
        rows = slice(e * SLOT_WIN, (e + 1) * SLOT_WIN)
        xs_ref[0, e, pl.ds(wa, SLOT_WIN), :] += res[rows].astype(BF16)
        gs_ref[0, e, pl.ds(wa, SLOT_WIN), :] += gres[rows]

    def overflow(e, carry):
        @pl.when(ovf_s[base + e] != 0)
        def _():
            r = rank_t_ref[0, pl.ds(e, 1), :]
            for k in range(CAP // SLOT_WIN):
                p = jnp.where(r - float(k * SLOT_WIN) == sub, 1.0, 0.0).astype(BF16)
                rows = slice(k * SLOT_WIN, (k + 1) * SLOT_WIN)
                xs_ref[0, e, rows, :] += _dot(p, h2_ref[0]).astype(BF16)
                gs_ref[0, e, rows, :] += gate_sum(_dot(p, gp_ref[0]))
        return carry

    lax.fori_loop(0, N_EXPERTS, overflow, 0)


def _gather(wa_s, ovf_s, rank_t, h2, gp):
    B, N, D = h2.shape
    E = N_EXPERTS
    tb = MOE_BLOCK
    return pl.pallas_call(
        _gather_kernel,
        grid_spec=pltpu.PrefetchScalarGridSpec(
            num_scalar_prefetch=2,
            grid=(B, N // tb),
            in_specs=[pl.BlockSpec((1, E, tb), lambda b, j, *_: (b, 0, j)),
                      pl.BlockSpec((1, tb, D), lambda b, j, *_: (b, j, 0)),
                      pl.BlockSpec((1, tb, 3 * LANES), lambda b, j, *_: (b, j, 0))],
            out_specs=[pl.BlockSpec((1, E, CAP, D), lambda b, j, *_: (b, 0, 0, 0)),
                       pl.BlockSpec((1, E, CAP, LANES), lambda b, j, *_: (b, 0, 0, 0))]),
        out_shape=[jax.ShapeDtypeStruct((B, E, CAP, D), BF16), jax.ShapeDtypeStruct((B, E, CAP, LANES), F32)],
        compiler_params=_cparams(("parallel", "arbitrary")),
        name="gather",
    )(wa_s, ovf_s, rank_t, h2, gp)


def _ffn_kernel(xs_ref, gs_ref, wg_ref, wu_ref, wd_ref, y_ref, wgb_ref, wub_ref, wdb_ref):
    e = pl.program_id(0)

    @pl.when(pl.program_id(1) == 0)
    def _():
        wgb_ref[...] = wg_ref[0].astype(BF16)
        wub_ref[...] = wu_ref[0].astype(BF16)
        wdb_ref[...] = wd_ref[0].astype(BF16)

    xs = xs_ref[0, 0]
    a = _dot(xs, wgb_ref[...])
    u = _dot(xs, wub_ref[...])
    hm = (a * jax.nn.sigmoid(a) * u).astype(BF16)
    lane = lax.broadcasted_iota(jnp.int32, gs_ref.shape[2:], 1)
    gate = jnp.sum(jnp.where(lane == e, gs_ref[0, 0], 0.0), axis=-1, keepdims=True)
    y_ref[0, 0] = (_dot(hm, wdb_ref[...]) * gate).astype(BF16)


def _ffn(xs, gs, wg, wu, wd):
    B, E, C, D = xs.shape
    F = wg.shape[2]
    return pl.pallas_call(
        _ffn_kernel,
        grid=(E, B),
        in_specs=[pl.BlockSpec((1, 1, C, D), lambda e, b: (b, e, 0, 0)),
                  pl.BlockSpec((1, 1, C, LANES), lambda e, b: (b, e, 0, 0)),
                  pl.BlockSpec((1, D, F), lambda e, b: (e, 0, 0)),
                  pl.BlockSpec((1, D, F), lambda e, b: (e, 0, 0)),
                  pl.BlockSpec((1, F, D), lambda e, b: (e, 0, 0))],
        out_specs=pl.BlockSpec((1, 1, C, D), lambda e, b: (b, e, 0, 0)),
        out_shape=jax.ShapeDtypeStruct((B, E, C, D), BF16),
        scratch_shapes=[pltpu.VMEM((D, F), BF16), pltpu.VMEM((D, F), BF16), pltpu.VMEM((F, D), BF16)],
        compiler_params=_cparams(("arbitrary", "arbitrary")),
        name="ffn",
    )(xs, gs, wg, wu, wd)


def _combine_kernel(wa_s, ovf_s, y_ref, rank_ref, wc_ref, spread_ref, x_ref, g2_ref, fg_ref, o_ref,
                    ycat_ref, acc_ref, *, final):
    b = pl.program_id(0)
    j = pl.program_id(1)
    base = (b * pl.num_programs(1) + j) * N_EXPERTS
    tb = x_ref.shape[1]
    kc = N_EXPERTS * SLOT_WIN

    for e in range(N_EXPERTS):
        wa = pl.multiple_of(wa_s[base + e], SLOT_ALIGN)
        ycat_ref[e * SLOT_WIN:(e + 1) * SLOT_WIN, :] = y_ref[0, e, pl.ds(wa, SLOT_WIN), :]
    rel = rank_ref[0] - wc_ref[0, pl.ds(j, 1), :]
    rel = jnp.where(rel >= 0.0, jnp.where(rel < float(SLOT_WIN), rel, float(SLOT_WIN)), float(SLOT_WIN))
    spread = _dot(rel.astype(BF16), spread_ref[...])
    col = (lax.broadcasted_iota(jnp.int32, (tb, kc), 1) % SLOT_WIN).astype(F32)
    pt = jnp.where(spread == col, 1.0, 0.0).astype(BF16)
    acc_ref[...] = _dot(pt, ycat_ref[...])

    def overflow(e, carry):
        @pl.when(ovf_s[base + e] != 0)
        def _():
            lane = lax.broadcasted_iota(jnp.int32, (tb, LANES), 1)
            rk = jnp.sum(jnp.where(lane == e, rank_ref[0], 0.0), axis=-1, keepdims=True)
            slot = lax.broadcasted_iota(jnp.int32, (tb, SLOT_WIN), 1).astype(F32)
            for k in range(CAP // SLOT_WIN):
                p = jnp.where(rk - float(k * SLOT_WIN) == slot, 1.0, 0.0).astype(BF16)
                acc_ref[...] += _dot(p, y_ref[0, e, k * SLOT_WIN:(k + 1) * SLOT_WIN, :])
        return carry

    lax.fori_loop(0, N_EXPERTS, overflow, 0)

    xn = x_ref[0] + g2_ref[0] * acc_ref[...]
    if final:
        xn = _rms(xn, fg_ref[...])
    o_ref[0] = xn


def _combine(wa_s, ovf_s, y, rank, wc, x, g2, fg, final):
    B, N, D = x.shape
    E = N_EXPERTS
    tb = MOE_BLOCK
    kc = E * SLOT_WIN
    spread = (jnp.arange(LANES)[:, None] == jnp.arange(kc)[None, :] // SLOT_WIN).astype(BF16)
    return pl.pallas_call(
        functools.partial(_combine_kernel, final=final),
        grid_spec=pltpu.PrefetchScalarGridSpec(
            num_scalar_prefetch=2,
            grid=(B, N // tb),
            in_specs=[pl.BlockSpec((1, E, CAP, D), lambda b, j, *_: (b, 0, 0, 0)),
                      pl.BlockSpec((1, tb, LANES), lambda b, j, *_: (b, j, 0)),
                      pl.BlockSpec((1, N // tb, LANES), lambda b, j, *_: (b, 0, 0)),
                      pl.BlockSpec((LANES, kc), lambda b, j, *_: (0, 0)),
                      pl.BlockSpec((1, tb, D), lambda b, j, *_: (b, j, 0)),
                      pl.BlockSpec((1, 1, D), lambda b, j, *_: (b, 0, 0)),
                      pl.BlockSpec((1, D), lambda b, j, *_: (0, 0))],
            out_specs=pl.BlockSpec((1, tb, D), lambda b, j, *_: (b, j, 0)),
            scratch_shapes=[pltpu.VMEM((kc, D), BF16), pltpu.VMEM((tb, D), F32)]),
        out_shape=jax.ShapeDtypeStruct((B, N, D), F32),
        compiler_params=_cparams(("parallel", "arbitrary")),
        name="combine",
    )(wa_s, ovf_s, y, rank, wc, spread, x, g2, fg)


def _moe(x, h2, aff, g2, wg, wu, wd, fg, final):
    rank, rank_t, gp, wa, wc = _topk(aff)
    wa_s = wa[:, :, :N_EXPERTS].astype(jnp.int32).reshape(-1)
    ovf_s = (wc[:, :, :N_EXPERTS] != wa[:, :, :N_EXPERTS]).astype(jnp.int32).reshape(-1)
    xs, gs = _gather(wa_s, ovf_s, rank_t, h2, gp)
    y = _ffn(xs, gs, wg, wu, wd)
    return _combine(wa_s, ovf_s, y, rank, wc, x, g2, fg, final)


def _pw1_kernel(x_ref, sh_ref, sc_ref, g_ref, w_ref, b_ref, u_ref):
    d = x_ref.shape[2]
    h = _rms(x_ref[0], g_ref[...]) * (1.0 + sc_ref[0]) + sh_ref[0]
    z = _dot(h.astype(BF16), w_ref[...]) + b_ref[...]
    u_ref[0] = z[:, :d] * jax.nn.sigmoid(z[:, d:])


def _pw1(x, sh, sc, g, w, b):
    B, N, D = x.shape
    tm = TOKEN_TILE
    vec = lambda: pl.BlockSpec((1, 1, D), lambda b, i: (b, 0, 0))
    return pl.pallas_call(
        _pw1_kernel,
        grid=(B, N // tm),
        in_specs=[pl.BlockSpec((1, tm, D), lambda b, i: (b, i, 0)), vec(), vec(),
                  pl.BlockSpec((1, D), lambda b, i: (0, 0)),
                  pl.BlockSpec((D, 2 * D), lambda b, i: (0, 0)),
                  pl.BlockSpec((1, 2 * D), lambda b, i: (0, 0))],
        out_specs=pl.BlockSpec((1, tm, D), lambda b, i: (b, i, 0)),
        out_shape=jax.ShapeDtypeStruct((B, N, D), F32),
        compiler_params=_cparams(("parallel", "parallel")),
        name="pw1",
    )(x, sh, sc, g, w, b)


def _conv_kernel(x_ref, u_ref, uprev_ref, unext_ref, wdw_ref, bdw_ref, lng_ref, lnb_ref, w2_ref, b2_ref,
                 g1_ref, g2n_ref, sh2_ref, sc2_ref, rwa_ref, rwb_ref, xo_ref, h2_ref, aff_ref, ext_ref):
    i = pl.program_id(1)
    nt = pl.num_programs(1)
    tm = x_ref.shape[1]
    hl = CONV_HALO
    ext_ref[0:hl] = jnp.where(i > 0, uprev_ref[0], 0.0)
    ext_ref[hl:hl + tm] = u_ref[0]
    ext_ref[hl + tm:] = jnp.where(i < nt - 1, unext_ref[0], 0.0)
    acc = None
    for k in range(CONV_WIDTH):
        off = hl + k - CONV_WIDTH // 2
        v = ext_ref[off:off + tm, :] * wdw_ref[k:k + 1, :]
        acc = v if acc is None else acc + v
    acc = acc + bdw_ref[...]
    mu = jnp.mean(acc, axis=-1, keepdims=True)
    cen = acc - mu
    var = jnp.mean(cen * cen, axis=-1, keepdims=True)
    yn = cen * lax.rsqrt(var + EPS) * lng_ref[...] + lnb_ref[...]
    yn = yn * jax.nn.sigmoid(yn)
    y = _dot(yn.astype(BF16), w2_ref[...]) + b2_ref[...]
    xn = x_ref[0] + g1_ref[0] * y
    _post(xn, g2n_ref, sh2_ref, sc2_ref, rwa_ref, rwb_ref, xo_ref, h2_ref, aff_ref)


def _conv(x, u, wdw, bdw, lng, lnb, w2, b2, g1, g2n, sh2, sc2, rwa, rwb):
    B, N, D = x.shape
    tm = TOKEN_TILE
    hb = tm // CONV_HALO
    nhb = N // CONV_HALO
    vec = lambda: pl.BlockSpec((1, 1, D), lambda b, i: (b, 0, 0))
    row = lambda: pl.BlockSpec((1, D), lambda b, i: (0, 0))
    pin, pout = _post_specs(D, tm)
    return pl.pallas_call(
        _conv_kernel,
        grid=(B, N // tm),
        in_specs=[pl.BlockSpec((1, tm, D), lambda b, i: (b, i, 0)),
                  pl.BlockSpec((1, tm, D), lambda b, i: (b, i, 0)),
                  pl.BlockSpec((1, CONV_HALO, D), lambda b, i: (b, jnp.maximum(i * hb - 1, 0), 0)),
                  pl.BlockSpec((1, CONV_HALO, D), lambda b, i: (b, jnp.minimum((i + 1) * hb, nhb - 1), 0)),
                  pl.BlockSpec(wdw.shape, lambda b, i: (0, 0)),
                  row(), row(), row(),
                  pl.BlockSpec((D, D), lambda b, i: (0, 0)),
                  row(), vec()] + pin,
        out_specs=pout,
        out_shape=_post_shapes(B, N, D),
        scratch_shapes=[pltpu.VMEM((tm + 2 * CONV_HALO, D), F32)],
        compiler_params=_cparams(("parallel", "parallel")),
        name="conv",
    )(x, u, u, u, wdw, bdw, lng, lnb, w2, b2, g1, g2n, sh2, sc2, rwa, rwb)


def _rope_tables(n):
    rows = n // GRID_W
    row = jnp.repeat(jnp.arange(rows), GRID_W).astype(F32)
    col = jnp.tile(jnp.arange(GRID_W), rows).astype(F32)
    per_axis = QK_ROPE // 2
    inv_freq = 1.0 / (ROPE_THETA ** (jnp.arange(0, per_axis, 2, dtype=F32) / per_axis))
    ang = jnp.stack([row[:, None] * inv_freq, col[:, None] * inv_freq], axis=1)
    cos, sin = jnp.cos(ang), jnp.sin(ang)
    cos32 = jnp.broadcast_to(cos[:, :, None, :], (n, 2, 2, QK_ROPE // 4)).reshape(n, QK_ROPE)
    sin32 = jnp.stack([-sin, sin], axis=2).reshape(n, QK_ROPE)
    return cos32, sin32


def _head_tables(cos32, sin32, lead, scale):
    n = cos32.shape[0]
    cosf = jnp.concatenate([jnp.full((n, QK_NOPE), lead, F32), cos32, jnp.zeros((n, QK_ROPE), F32)], axis=1)
    sinf = jnp.concatenate([jnp.zeros((n, QK_NOPE), F32), sin32, jnp.zeros((n, QK_ROPE), F32)], axis=1)
    return cosf * scale, sinf * scale


def _layer0_weights(mix_w_in, w_uq, w_ukv):
    D = mix_w_in.shape[0]
    perm = jnp.arange(QK_ROPE) ^ (QK_ROPE // 4)
    rope_cols = mix_w_in[:, KV_OFF + KV_LORA:]
    wmix = jnp.concatenate([mix_w_in[:, :KV_OFF + KV_LORA], jnp.zeros((D, QK_NOPE), F32),
                            rope_cols, rope_cols[:, perm]], axis=1).astype(BF16)
    wq3 = w_uq.reshape(Q_LORA, MLA_HEADS, QK_DIM)
    wq = jnp.concatenate([wq3, wq3[:, :, QK_NOPE:][:, :, perm]], axis=2).reshape(Q_LORA, MLA_HEADS * HEAD_PAD)
    wkv3 = w_ukv.reshape(KV_LORA, MLA_HEADS, QK_NOPE + V_DIM)
    zpad = jnp.zeros((KV_LORA, MLA_HEADS, HEAD_PAD - QK_NOPE), F32)
    wk = jnp.concatenate([wkv3[:, :, :QK_NOPE], zpad], axis=2)
    wv_even = jnp.concatenate([wkv3[:, :, QK_NOPE:], zpad], axis=2)
    wv_odd = jnp.concatenate([zpad, wkv3[:, :, QK_NOPE:]], axis=2)
    odd = (jnp.arange(MLA_HEADS) % 2 == 1)[None, :, None]
    wv = jnp.where(odd, wv_odd, wv_even)
    wkv = jnp.concatenate([wk.reshape(KV_LORA, -1), wv.reshape(KV_LORA, -1)], axis=1)
    return wmix, wq.astype(BF16), wkv.astype(BF16)


def _router_weights(rw):
    D, E = rw.shape
    rwp = jnp.concatenate([rw, jnp.zeros((D, LANES - E), F32)], axis=1)
    hi = rwp.astype(BF16)
    lo = (rwp - hi.astype(F32)).astype(BF16)
    return jnp.concatenate([hi, lo], axis=1), hi


def kernel(x, c, ctx, c_ctx, ada_w, ada_b, norm1_g, norm2_g, mix_w_in, pool_w, pool_scale, q_norm_g, kv_norm_g, w_uq, w_ukv, mix_w_out, conv_w_pw1, conv_b_pw1, conv_w_dw, conv_b_dw, conv_ln_g, conv_ln_b, conv_w_pw2, conv_b_pw2, router_w, exp_wg, exp_wu, exp_wd, final_g):
    B, N, D = x.shape
    LC = ctx.shape[1]

    cc = jnp.concatenate([c, c_ctx[None, :], jnp.zeros((16 - B - 1, D), F32)], axis=0)
    mods = _ada(cc, ada_w, ada_b)

    def mod(l, k):
        return mods[l, :B, k * D:(k + 1) * D].reshape(B, 1, D)

    def mod_ctx(l, k):
        return jnp.broadcast_to(mods[l, B, k * D:(k + 1) * D].reshape(1, 1, D), (B, 1, D))

    row = lambda v: v.reshape(1, -1)

    wmix, wq, wkv = _layer0_weights(mix_w_in[0], w_uq[0], w_ukv[0])
    qscale = (QK_DIM ** -0.5) * math.log2(math.e)
    cos32, sin32 = _rope_tables(N)
    cq, sq = _head_tables(cos32, sin32, 1.0, qscale)
    ck, sk = _head_tables(cos32, sin32, 0.0, 1.0)
    one32, zero32 = jnp.ones((LC, QK_ROPE), F32), jnp.zeros((LC, QK_ROPE), F32)
    cqc, sqc = _head_tables(one32, zero32, 1.0, qscale)
    ckc, skc = _head_tables(one32, zero32, 0.0, 1.0)
    lw = (row(norm1_g[0]), wmix, row(q_norm_g[0]), wq, row(kv_norm_g[0]), wkv)
    q, k, v, zp = _mixin(x, mod(0, 0), mod(0, 1), *lw, cq, sq, ck, sk, TOKEN_TILE)
    _, kc, vc, _ = _mixin(ctx, mod_ctx(0, 0), mod_ctx(0, 1), *lw, cqc, sqc, ckc, skc, LC)
    attn = _attn(q, k, v, kc, vc)
    rwa, rwb = _router_weights(router_w[0])
    x, h2, aff = _mixout(x, zp, attn, pool_w[0].astype(BF16), row(pool_scale[0]), mix_w_out[0].astype(BF16),
                         mod(0, 2), row(norm2_g[0]), mod(0, 3), mod(0, 4), rwa, rwb)
    x = _moe(x, h2, aff, mod(0, 5), exp_wg[0], exp_wu[0], exp_wd[0], row(final_g), False)

    u = _pw1(x, mod(1, 0), mod(1, 1), row(norm1_g[1]), conv_w_pw1[0].astype(BF16), row(conv_b_pw1[0]))
    wdw = jnp.concatenate([conv_w_dw[0], jnp.zeros((1, D), F32)], axis=0)
    rwa, rwb = _router_weights(router_w[1])
    x, h2, aff = _conv(x, u, wdw, row(conv_b_dw[0]), row(conv_ln_g[0]), row(conv_ln_b[0]),
                       conv_w_pw2[0].astype(BF16), row(conv_b_pw2[0]),
                       mod(1, 2), row(norm2_g[1]), mod(1, 3), mod(1, 4), rwa, rwb)
    return _moe(x, h2, aff, mod(1, 5), exp_wg[1], exp_wu[1], exp_wd[1], row(final_g), True)
```

```python
import functools
import math

import jax
import jax.numpy as jnp
from jax import lax
from jax.experimental import pallas as pl
from jax.experimental.pallas import tpu as pltpu

F32 = jnp.float32
BF16 = jnp.bfloat16

D_MODEL = 1024
BATCH = 8
SEQ = 4096
CTX_LEN = 256
GRID_W = 64
POOL_WINDOWS = (2, 4, 8, 16)
POOL_GROUP_DIM = 128
POOL_DIM = 512
MLA_HEADS = 8
QK_NOPE = 64
QK_ROPE = 32
V_DIM = 64
Q_LORA = 256
KV_LORA = 128
QK_DIM = QK_NOPE + QK_ROPE
KV_OFF = POOL_DIM + Q_LORA
ROPE_THETA = 10000.0
CONV_WIDTH = 31
N_EXPERTS = 16
CAP = 2 * SEQ // N_EXPERTS
EPS = 1e-6

LANES = 128
HEAD_PAD = 128
POOL_HALO = 8
CONV_HALO = 16
CONV_ROWS = 64
TOKEN_TILE = 512
Q_TILE = 512
ATTN_KEY_CHUNK = 1024
MOE_BLOCK = 256
SLOT_WIN = 64
SLOT_ALIGN = 16
WIN_OVERFLOW = 4096.0
VMEM_LIMIT = 52 * 1024 * 1024


def _cparams(sem):
    return pltpu.CompilerParams(dimension_semantics=sem, vmem_limit_bytes=VMEM_LIMIT)


def _rms(xf, g):
    ms = jnp.mean(xf * xf, axis=-1, keepdims=True)
    return xf * lax.rsqrt(ms + EPS) * g


def _dot(a, b):
    return jnp.dot(a, b, preferred_element_type=F32)


def _dot_nt(a, b):
    return lax.dot_general(a, b, (((1,), (1,)), ((), ())), preferred_element_type=F32)


def _ada_kernel(c_ref, w_ref, b_ref, o_ref):
    cv = c_ref[...]
    s = cv * jax.nn.sigmoid(cv)
    o_ref[0] = jnp.dot(s, w_ref[0], precision=lax.Precision.HIGHEST,
                       preferred_element_type=F32) + b_ref[0]


def _ada(cc, ada_w, ada_b):
    L, D, D6 = ada_w.shape
    tn = 1536
    return pl.pallas_call(
        _ada_kernel,
        grid=(L, D6 // tn),
        in_specs=[pl.BlockSpec((16, D), lambda l, j: (0, 0)),
                  pl.BlockSpec((1, D, tn), lambda l, j: (l, 0, j)),
                  pl.BlockSpec((1, 1, tn), lambda l, j: (l, 0, j))],
        out_specs=pl.BlockSpec((1, 16, tn), lambda l, j: (l, 0, j)),
        out_shape=jax.ShapeDtypeStruct((L, 16, D6), F32),
        compiler_params=_cparams(("parallel", "parallel")),
        name="ada",
    )(cc, ada_w, ada_b.reshape(L, 1, D6))


def _mixin_kernel(x_ref, sh_ref, sc_ref, g_ref, wmix_ref, qg_ref, wq_ref, kvg_ref, wk_ref, wvt_ref,
                  cq_ref, sq_ref, ck_ref, sk_ref, q_ref, k_ref, vt_ref, zp_ref):
    h = _rms(x_ref[0], g_ref[...]) * (1.0 + sc_ref[0]) + sh_ref[0]
    z = _dot(h.astype(BF16), wmix_ref[...])
    zp_ref[0] = z[:, :POOL_DIM]
    zq = _rms(z[:, POOL_DIM:KV_OFF], qg_ref[...]).astype(BF16)
    qa = _dot(zq, wq_ref[...])
    cq = cq_ref[...]
    sq = sq_ref[...]
    for hh in range(MLA_HEADS):
        qh = qa[:, hh * HEAD_PAD:(hh + 1) * HEAD_PAD]
        q_ref[0, hh] = (qh * cq + pltpu.roll(qh, HEAD_PAD - QK_ROPE, 1) * sq).astype(BF16)
    zkv = _rms(z[:, KV_OFF:KV_OFF + KV_LORA], kvg_ref[...]).astype(BF16)
    ka = _dot(zkv, wk_ref[...])
    zr = z[:, KV_OFF + KV_LORA:]
    kr = zr * ck_ref[...] + pltpu.roll(zr, HEAD_PAD - QK_ROPE, 1) * sk_ref[...]
    for hh in range(MLA_HEADS):
        k_ref[0, hh] = (ka[:, hh * HEAD_PAD:(hh + 1) * HEAD_PAD] + kr).astype(BF16)
    vt_ref[0] = _dot_nt(wvt_ref[...], zkv).astype(BF16)


def _mixin(x, sh, sc, g, wmix, qg, wq, kvg, wk, wvt, cq, sq, ck, sk, tm):
    B, N, D = x.shape
    H = MLA_HEADS
    vec = lambda: pl.BlockSpec((1, 1, D), lambda b, i: (b, 0, 0))
    full = lambda a: pl.BlockSpec(a.shape, lambda b, i: (0,) * a.ndim)
    tab = lambda: pl.BlockSpec((tm, LANES), lambda b, i: (i, 0))
    hd = lambda: pl.BlockSpec((1, H, tm, HEAD_PAD), lambda b, i: (b, 0, i, 0))
    return pl.pallas_call(
        _mixin_kernel,
        grid=(B, N // tm),
        in_specs=[pl.BlockSpec((1, tm, D), lambda b, i: (b, i, 0)), vec(), vec(), full(g), full(wmix),
                  full(qg), full(wq), full(kvg), full(wk), full(wvt), tab(), tab(), tab(), tab()],
        out_specs=[hd(), hd(), pl.BlockSpec((1, H * V_DIM, tm), lambda b, i: (b, 0, i)),
                   pl.BlockSpec((1, tm, POOL_DIM), lambda b, i: (b, i, 0))],
        out_shape=[jax.ShapeDtypeStruct((B, H, N, HEAD_PAD), BF16)] * 2
        + [jax.ShapeDtypeStruct((B, H * V_DIM, N), BF16), jax.ShapeDtypeStruct((B, N, POOL_DIM), F32)],
        compiler_params=_cparams(("parallel", "parallel")),
        name="mixin",
    )(x, sh, sc, g, wmix, qg, wq, kvg, wk, wvt, cq, sq, ck, sk)


def _col_reduce(x, op):
    for group in (256, 64, 8):
        if x.shape[0] > group:
            x = op(x.reshape(x.shape[0] // group, group, x.shape[1]), axis=0)
    return op(x, axis=0, keepdims=True)


def _attn_kernel(q_ref, k_ref, vt_ref, kc_ref, vtc_ref, o_ref):
    tq = q_ref.shape[2]
    ck = ATTN_KEY_CHUNK
    nchunk = 1 + k_ref.shape[2] // ck

    def score_chunk(j, c):
        if c == 0:
            return _dot_nt(kc_ref[0, j], q_ref[0, j])
        return _dot_nt(k_ref[0, j, (c - 1) * ck:c * ck, :], q_ref[0, j])

    def value_chunk(j, c):
        rows = slice(j * V_DIM, (j + 1) * V_DIM)
        if c == 0:
            return vtc_ref[0, rows, :]
        return vt_ref[0, rows, (c - 1) * ck:c * ck]

    def col_max(chunks):
        m = None
        for s in chunks:
            r = _col_reduce(s, jnp.max)
            m = r if m is None else jnp.maximum(m, r)
        return m

    def consume(j, c, s, m, acc, l):
        p = jnp.exp2(s - m)
        return acc + _dot(value_chunk(j, c), p.astype(BF16)), l + _col_reduce(p, jnp.sum)

    zero = (jnp.zeros((V_DIM, tq), F32), jnp.zeros((1, tq), F32))
    s0 = [score_chunk(0, c) for c in range(nchunk)]
    m0 = col_max(s0)
    s1 = []
    acc0, l0 = zero
    for c in range(nchunk):
        s1.append(score_chunk(1, c))
        acc0, l0 = consume(0, c, s0[c], m0, acc0, l0)
    m1 = col_max(s1)
    acc1, l1 = zero
    for c in range(nchunk):
        acc1, l1 = consume(1, c, s1[c], m1, acc1, l1)
    o_ref[0] = jnp.concatenate([acc0 / l0, acc1 / l1], axis=0).T.astype(BF16)


def _attn(q, k, vt, kc, vtc):
    B, H, N, P = q.shape
    LC = kc.shape[2]
    tq = Q_TILE
    kspec = lambda n: pl.BlockSpec((1, 2, n, P), lambda b, p, i: (b, p, 0, 0))
    vspec = lambda n: pl.BlockSpec((1, 2 * V_DIM, n), lambda b, p, i: (b, p, 0))
    return pl.pallas_call(
        _attn_kernel,
        grid=(B, H // 2, N // tq),
        in_specs=[pl.BlockSpec((1, 2, tq, P), lambda b, p, i: (b, p, i, 0)),
                  kspec(N), vspec(N), kspec(LC), vspec(LC)],
        out_specs=pl.BlockSpec((1, tq, LANES), lambda b, p, i: (b, i, p)),
        out_shape=jax.ShapeDtypeStruct((B, N, MLA_HEADS * V_DIM), BF16),
        compiler_params=_cparams(("parallel", "parallel", "parallel")),
        name="attn",
    )(q, k, vt, kc, vtc)


def _post(xn, g2n_ref, sh2_ref, sc2_ref, rwa_ref, rwb_ref, xo_ref, h2_ref, aff_ref):
    xo_ref[0] = xn
    h2 = _rms(xn, g2n_ref[...]) * (1.0 + sc2_ref[0]) + sh2_ref[0]
    hi = h2.astype(BF16)
    lo = (h2 - hi.astype(F32)).astype(BF16)
    h2_ref[0] = hi
    la = _dot(hi, rwa_ref[...])
    lb = _dot(lo, rwb_ref[...])
    logits = la[:, :LANES] + la[:, LANES:] + lb
    lane = lax.broadcasted_iota(jnp.int32, logits.shape, 1)
    logits = jnp.where(lane < N_EXPERTS, logits, -1e30)
    m = jnp.max(logits, axis=-1, keepdims=True)
    e = jnp.exp(logits - m)
    aff_ref[0] = e / jnp.sum(e, axis=-1, keepdims=True)


def _post_specs(D, tm):
    vec = lambda: pl.BlockSpec((1, 1, D), lambda b, i: (b, 0, 0))
    in_specs = [pl.BlockSpec((1, D), lambda b, i: (0, 0)), vec(), vec(),
                pl.BlockSpec((D, 2 * LANES), lambda b, i: (0, 0)),
                pl.BlockSpec((D, LANES), lambda b, i: (0, 0))]
    out_specs = [pl.BlockSpec((1, tm, D), lambda b, i: (b, i, 0)),
                 pl.BlockSpec((1, tm, D), lambda b, i: (b, i, 0)),
                 pl.BlockSpec((1, tm, LANES), lambda b, i: (b, i, 0))]
    return in_specs, out_specs


def _post_shapes(B, N, D):
    return [jax.ShapeDtypeStruct((B, N, D), F32), jax.ShapeDtypeStruct((B, N, D), BF16),
            jax.ShapeDtypeStruct((B, N, LANES), F32)]


def _mixout_kernel(x_ref, zp_ref, zprev_ref, znext_ref, attn_ref, pw_ref, ps_ref, wout_ref, g1_ref,
                   g2n_ref, sh2_ref, sc2_ref, rwa_ref, rwb_ref, xo_ref, h2_ref, aff_ref, ext_ref):
    i = pl.program_id(1)
    nt = pl.num_programs(1)
    tm = x_ref.shape[1]
    hl = POOL_HALO
    ext_ref[0:hl] = jnp.where(i > 0, zprev_ref[0], 0.0)
    ext_ref[hl:hl + tm] = zp_ref[0]
    ext_ref[hl + tm:] = jnp.where(i < nt - 1, znext_ref[0], 0.0)
    t = i * tm + lax.broadcasted_iota(jnp.int32, (tm, 1), 0)
    ys = []
    for g, w in enumerate(POOL_WINDOWS):
        cols = slice(g * POOL_GROUP_DIM, (g + 1) * POOL_GROUP_DIM)
        s = ext_ref[:, cols]
        n = s.shape[0]
        span = 1
        while span < w:
            s = s + pltpu.roll(s, n - span, 0)
            span *= 2
        first = hl - w // 2
        s = (pltpu.roll(s, n - first, 0) if first else s)[:tm]
        lo = jnp.maximum(t - w // 2, 0)
        hi = jnp.minimum(t - w // 2 + w, SEQ)
        d = s / (hi - lo).astype(F32) - ext_ref[hl:hl + tm, cols]
        ys.append(_dot(d.astype(BF16), pw_ref[g]))
    pool = (jnp.concatenate(ys, axis=-1) * ps_ref[...]).astype(BF16)
    y = _dot(pool, wout_ref[0:POOL_DIM]) + _dot(attn_ref[0], wout_ref[POOL_DIM:])
    xn = x_ref[0] + g1_ref[0] * y
    _post(xn, g2n_ref, sh2_ref, sc2_ref, rwa_ref, rwb_ref, xo_ref, h2_ref, aff_ref)


def _mixout(x, zp, attn, pw, ps, wout, g1, g2n, sh2, sc2, rwa, rwb):
    B, N, D = x.shape
    tm = TOKEN_TILE
    hb = tm // POOL_HALO
    nhb = N // POOL_HALO
    vec = lambda: pl.BlockSpec((1, 1, D), lambda b, i: (b, 0, 0))
    pin, pout = _post_specs(D, tm)
    return pl.pallas_call(
        _mixout_kernel,
        grid=(B, N // tm),
        in_specs=[pl.BlockSpec((1, tm, D), lambda b, i: (b, i, 0)),
                  pl.BlockSpec((1, tm, POOL_DIM), lambda b, i: (b, i, 0)),
                  pl.BlockSpec((1, POOL_HALO, POOL_DIM), lambda b, i: (b, jnp.maximum(i * hb - 1, 0), 0)),
                  pl.BlockSpec((1, POOL_HALO, POOL_DIM),
                               lambda b, i: (b, jnp.minimum((i + 1) * hb, nhb - 1), 0)),
                  pl.BlockSpec((1, tm, POOL_DIM), lambda b, i: (b, i, 0)),
                  pl.BlockSpec(pw.shape, lambda b, i: (0, 0, 0)),
                  pl.BlockSpec((1, POOL_DIM), lambda b, i: (0, 0)),
                  pl.BlockSpec(wout.shape, lambda b, i: (0, 0)),
                  vec()] + pin,
        out_specs=pout,
        out_shape=_post_shapes(B, N, D),
        scratch_shapes=[pltpu.VMEM((tm + 2 * POOL_HALO, POOL_DIM), F32)],
        compiler_params=_cparams(("parallel", "parallel")),
        name="mixout",
    )(x, zp, zp, zp, attn, pw, ps, wout, g1, g2n, sh2, sc2, rwa, rwb)


def _topk_kernel(aff_ref, rank_ref, rank_t_ref, gp_ref, wa_ref, wc_ref):
    n = aff_ref.shape[1]
    nchunk = n // LANES

    def search(it, thr):
        bits = pltpu.bitcast(aff_ref[0], jnp.int32)
        cand = thr | (jnp.int32(1) << (30 - it))
        cnt = jnp.sum(jnp.where(bits >= cand, 1.0, 0.0), axis=0, keepdims=True)
        return jnp.where(cnt >= CAP, cand, thr)

    thr = lax.fori_loop(0, 31, search, jnp.zeros((1, LANES), jnp.int32))

    ri = lax.broadcasted_iota(jnp.int32, (LANES, LANES), 0)
    ci = lax.broadcasted_iota(jnp.int32, (LANES, LANES), 1)
    ltri = jnp.where(ri > ci, 1.0, 0.0).astype(BF16)

    def prefix(mask):
        run = jnp.zeros((1, LANES), F32)
        outs = []
        for c in range(nchunk):
            mc = mask[c * LANES:(c + 1) * LANES]
            outs.append(_dot(ltri, mc.astype(BF16)) + run)
            run = run + jnp.sum(mc, axis=0, keepdims=True)
        return jnp.concatenate(outs, axis=0), run

    a = aff_ref[0]
    bits = pltpu.bitcast(a, jnp.int32)
    gt = jnp.where(bits > thr, 1.0, 0.0)
    eq = jnp.where(bits == thr, 1.0, 0.0)
    need = CAP - jnp.sum(gt, axis=0, keepdims=True)
    pe, _ = prefix(eq)
    sel = gt + eq * jnp.where(pe < need, 1.0, 0.0)
    pref, total = prefix(sel)
    rank = jnp.where(sel > 0.0, pref, -1.0)
    rank_ref[0] = rank
    for c in range(nchunk):
        rank_t_ref[0, :, c * LANES:(c + 1) * LANES] = rank[c * LANES:(c + 1) * LANES].T[:N_EXPERTS]
    gate = jnp.where(sel > 0.0, a, 0.0)
    g_hi = gate.astype(BF16)
    r1 = gate - g_hi.astype(F32)
    g_mid = r1.astype(BF16)
    g_lo = (r1 - g_mid.astype(F32)).astype(BF16)
    gp_ref[0] = jnp.concatenate([g_hi, g_mid, g_lo], axis=-1)
    lo =[pref[j * MOE_BLOCK:j * MOE_BLOCK + 1] for j in range(n // MOE_BLOCK)] + [total]
    wa, wc = [], []
    for j in range(n // MOE_BLOCK):
        start = jnp.minimum(jnp.floor(lo[j] * (1.0 / SLOT_ALIGN)) * SLOT_ALIGN, float(CAP - SLOT_WIN))
        wa.append(start)
        wc.append(start + jnp.where(lo[j + 1] > start + SLOT_WIN, WIN_OVERFLOW, 0.0))
    wa_ref[0] = jnp.concatenate(wa, axis=0)
    wc_ref[0] = jnp.concatenate(wc, axis=0)


def _topk(aff):
    B, N, E = aff.shape
    nj = N // MOE_BLOCK
    spec = lambda: pl.BlockSpec((1, N, E), lambda b: (b, 0, 0))
    wspec = lambda: pl.BlockSpec((1, nj, E), lambda b: (b, 0, 0))
    return pl.pallas_call(
        _topk_kernel,
        grid=(B,),
        in_specs=[spec()],
        out_specs=[spec(), pl.BlockSpec((1, N_EXPERTS, N), lambda b: (b, 0, 0)),
                   pl.BlockSpec((1, N, 3 * E), lambda b: (b, 0, 0)), wspec(), wspec()],
        out_shape=[jax.ShapeDtypeStruct((B, N, E), F32), jax.ShapeDtypeStruct((B, N_EXPERTS, N), F32),
                   jax.ShapeDtypeStruct((B, N, 3 * E), BF16),
                   jax.ShapeDtypeStruct((B, nj, E), F32), jax.ShapeDtypeStruct((B, nj, E), F32)],
        compiler_params=_cparams(("parallel",)),
        name="topk",
    )(aff)


def _gather_kernel(wa_s, ovf_s, rank_t_ref, h2_ref, gp_ref, xs_ref, gs_ref):
    b = pl.program_id(0)
    j = pl.program_id(1)
    base = (b * pl.num_programs(1) + j) * N_EXPERTS
    tb = h2_ref.shape[1]

    @pl.when(j == 0)
    def _():
        xs_ref[...] = jnp.zeros_like(xs_ref)
        gs_ref[...] = jnp.zeros_like(gs_ref)

    def gate_sum(g):
        return g[:, :LANES] + g[:, LANES:2 * LANES] + g[:, 2 * LANES:]

    sub = lax.broadcasted_iota(jnp.int32, (SLOT_WIN, tb), 0).astype(F32)
    ps = []
    for e in range(N_EXPERTS):
        wcmp = (wa_s[base + e] + ovf_s[base + e] * int(WIN_OVERFLOW)).astype(F32)
        ps.append(jnp.where(rank_t_ref[0, e:e + 1, :] - wcmp == sub, 1.0, 0.0).astype(BF16))
    pcat = jnp.concatenate(ps, axis=0)
    res = _dot(pcat, h2_ref[0])
    gres = gate_sum(_dot(pcat, gp_ref[0]))
    for e in range(N_EXPERTS):
        wa = pl.multiple_of(wa_s[base + e], SLOT_ALIGN)
        rows = slice(e * SLOT_WIN, (e + 1) * SLOT_WIN)
        xs_ref[0, e, pl.ds(wa, SLOT_WIN), :] += res[rows].astype(BF16)
        gs_ref[0, e, pl.ds(wa, SLOT_WIN), :] += gres[rows]

    def overflow(e, carry):
        @pl.when(ovf_s[base + e] != 0)
        def _():
            r = rank_t_ref[0, pl.ds(e, 1), :]
            for k in range(CAP // SLOT_WIN):
                p = jnp.where(r - float(k * SLOT_WIN) == sub, 1.0, 0.0).astype(BF16)
                rows = slice(k * SLOT_WIN, (k + 1) * SLOT_WIN)
                xs_ref[0, e, rows, :] += _dot(p, h2_ref[0]).astype(BF16)
                gs_ref[0, e, rows, :] += gate_sum(_dot(p, gp_ref[0]))
        return carry

    lax.fori_loop(0, N_EXPERTS, overflow, 0)


def _gather(wa_s, ovf_s, rank_t, h2, gp):
    B, N, D = h2.shape
    E = N_EXPERTS
    tb = MOE_BLOCK
    return pl.pallas_call(
        _gather_kernel,
        grid_spec=pltpu.PrefetchScalarGridSpec(
            num_scalar_prefetch=2,
            grid=(B, N // tb),
            in_specs=[pl.BlockSpec((1, E, tb), lambda b, j, *_: (b, 0, j)),
                      pl.BlockSpec((1, tb, D), lambda b, j, *_: (b, j, 0)),
                      pl.BlockSpec((1, tb, 3 * LANES), lambda b, j, *_: (b, j, 0))],
            out_specs=[pl.BlockSpec((1, E, CAP, D), lambda b, j, *_: (b, 0, 0, 0)),
                       pl.BlockSpec((1, E, CAP, LANES), lambda b, j, *_: (b, 0, 0, 0))]),
        out_shape=[jax.ShapeDtypeStruct((B, E, CAP, D), BF16), jax.ShapeDtypeStruct((B, E, CAP, LANES), F32)],
        compiler_params=_cparams(("parallel", "arbitrary")),
        name="gather",
    )(wa_s, ovf_s, rank_t, h2, gp)


def _ffn_kernel(xs_ref, gs_ref, wg_ref, wu_ref, wd_ref, y_ref, wgb_ref, wub_ref, wdb_ref):
    e = pl.program_id(0)

    @pl.when(pl.program_id(1) == 0)
    def _():
        wgb_ref[...] = wg_ref[0, 0].astype(BF16)
        wub_ref[...] = wu_ref[0, 0].astype(BF16)
        wdb_ref[...] = wd_ref[0, 0].astype(BF16)

    xs = xs_ref[0, 0]
    a = _dot(xs, wgb_ref[...])
    u = _dot(xs, wub_ref[...])
    hm = (a * jax.nn.sigmoid(a) * u).astype(BF16)
    lane = lax.broadcasted_iota(jnp.int32, gs_ref.shape[2:], 1)
    gate = jnp.sum(jnp.where(lane == e, gs_ref[0, 0], 0.0), axis=-1, keepdims=True)
    y_ref[0, 0] = (_dot(hm, wdb_ref[...]) * gate).astype(BF16)


def _ffn(xs, gs, wg, wu, wd, layer):
    B, E, C, D = xs.shape
    F = wg.shape[3]
    return pl.pallas_call(
        _ffn_kernel,
        grid=(E, B),
        in_specs=[pl.BlockSpec((1, 1, C, D), lambda e, b: (b, e, 0, 0)),
                  pl.BlockSpec((1, 1, C, LANES), lambda e, b: (b, e, 0, 0)),
                  pl.BlockSpec((1, 1, D, F), lambda e, b: (layer, e, 0, 0)),
                  pl.BlockSpec((1, 1, D, F), lambda e, b: (layer, e, 0, 0)),
                  pl.BlockSpec((1, 1, F, D), lambda e, b: (layer, e, 0, 0))],
        out_specs=pl.BlockSpec((1, 1, C, D), lambda e, b: (b, e, 0, 0)),
        out_shape=jax.ShapeDtypeStruct((B, E, C, D), BF16),
        scratch_shapes=[pltpu.VMEM((D, F), BF16), pltpu.VMEM((D, F), BF16), pltpu.VMEM((F, D), BF16)],
        compiler_params=_cparams(("arbitrary", "arbitrary")),
        name="ffn",
    )(xs, gs, wg, wu, wd)


def _combine_kernel(wa_s, ovf_s, y_ref, rank_ref, wc_ref, spread_ref, x_ref, g2_ref, fg_ref, o_ref,
                    ycat_ref, acc_ref, *, final):
    b = pl.program_id(0)
    j = pl.program_id(1)
    base = (b * pl.num_programs(1) + j) * N_EXPERTS
    tb = x_ref.shape[1]
    kc = N_EXPERTS * SLOT_WIN

    for e in range(N_EXPERTS):
        wa = pl.multiple_of(wa_s[base + e], SLOT_ALIGN)
        ycat_ref[e * SLOT_WIN:(e + 1) * SLOT_WIN, :] = y_ref[0, e, pl.ds(wa, SLOT_WIN), :]
    rel = rank_ref[0] - wc_ref[0, pl.ds(j, 1), :]
    rel = jnp.where(rel >= 0.0, jnp.where(rel < float(SLOT_WIN), rel, float(SLOT_WIN)), float(SLOT_WIN))
    spread = _dot(rel.astype(BF16), spread_ref[...])
    col = (lax.broadcasted_iota(jnp.int32, (tb, kc), 1) % SLOT_WIN).astype(F32)
    pt = jnp.where(spread == col, 1.0, 0.0).astype(BF16)
    acc_ref[...] = _dot(pt, ycat_ref[...])

    def overflow(e, carry):
        @pl.when(ovf_s[base + e] != 0)
        def _():
            lane = lax.broadcasted_iota(jnp.int32, (tb, LANES), 1)
            rk = jnp.sum(jnp.where(lane == e, rank_ref[0], 0.0), axis=-1, keepdims=True)
            slot = lax.broadcasted_iota(jnp.int32, (tb, SLOT_WIN), 1).astype(F32)
            for k in range(CAP // SLOT_WIN):
                p = jnp.where(rk - float(k * SLOT_WIN) == slot, 1.0, 0.0).astype(BF16)
                acc_ref[...] += _dot(p, y_ref[0, e, k * SLOT_WIN:(k + 1) * SLOT_WIN, :])
        return carry

    lax.fori_loop(0, N_EXPERTS, overflow, 0)

    xn = x_ref[0] + g2_ref[0] * acc_ref[...]
    if final:
        xn = _rms(xn, fg_ref[...])
    o_ref[0] = xn


def _combine(wa_s, ovf_s, y, rank, wc, x, g2, fg, final):
    B, N, D = x.shape
    E = N_EXPERTS
    tb = MOE_BLOCK
    kc = E * SLOT_WIN
    spread = (jnp.arange(LANES)[:, None] == jnp.arange(kc)[None, :] // SLOT_WIN).astype(BF16)
    return pl.pallas_call(
        functools.partial(_combine_kernel, final=final),
        grid_spec=pltpu.PrefetchScalarGridSpec(
            num_scalar_prefetch=2,
            grid=(B, N // tb),
            in_specs=[pl.BlockSpec((1, E, CAP, D), lambda b, j, *_: (b, 0, 0, 0)),
                      pl.BlockSpec((1, tb, LANES), lambda b, j, *_: (b, j, 0)),
                      pl.BlockSpec((1, N // tb, LANES), lambda b, j, *_: (b, 0, 0)),
                      pl.BlockSpec((LANES, kc), lambda b, j, *_: (0, 0)),
                      pl.BlockSpec((1, tb, D), lambda b, j, *_: (b, j, 0)),
                      pl.BlockSpec((1, 1, D), lambda b, j, *_: (b, 0, 0)),
                      pl.BlockSpec((1, D), lambda b, j, *_: (0, 0))],
            out_specs=pl.BlockSpec((1, tb, D), lambda b, j, *_: (b, j, 0)),
            scratch_shapes=[pltpu.VMEM((kc, D), BF16), pltpu.VMEM((tb, D), F32)]),
        out_shape=jax.ShapeDtypeStruct((B, N, D), F32),
        compiler_params=_cparams(("parallel", "arbitrary")),
        name="combine",
    )(wa_s, ovf_s, y, rank, wc, spread, x, g2, fg)


def _moe(x, h2, aff, g2, wg, wu, wd, layer, fg, final):
    rank, rank_t, gp, wa, wc = _topk(aff)
    wa_s = wa[:, :, :N_EXPERTS].astype(jnp.int32).reshape(-1)
    ovf_s = (wc[:, :, :N_EXPERTS] != wa[:, :, :N_EXPERTS]).astype(jnp.int32).reshape(-1)
    xs, gs = _gather(wa_s, ovf_s, rank_t, h2, gp)
    y = _ffn(xs, gs, wg, wu, wd, layer)
    return _combine(wa_s, ovf_s, y, rank, wc, x, g2, fg, final)


def _pw1_kernel(x_ref, sh_ref, sc_ref, g_ref, w_ref, b_ref, u_ref):
    d = x_ref.shape[2]
    h = _rms(x_ref[0], g_ref[...]) * (1.0 + sc_ref[0]) + sh_ref[0]
    z = _dot(h.astype(BF16), w_ref[...]) + b_ref[...]
    u_ref[0] = z[:, :d] * jax.nn.sigmoid(z[:, d:])


def _pw1(x, sh, sc, g, w, b):
    B, N, D = x.shape
    tm = TOKEN_TILE
    vec = lambda: pl.BlockSpec((1, 1, D), lambda b, i: (b, 0, 0))
    return pl.pallas_call(
        _pw1_kernel,
        grid=(B, N // tm),
        in_specs=[pl.BlockSpec((1, tm, D), lambda b, i: (b, i, 0)), vec(), vec(),
                  pl.BlockSpec((1, D), lambda b, i: (0, 0)),
                  pl.BlockSpec((D, 2 * D), lambda b, i: (0, 0)),
                  pl.BlockSpec((1, 2 * D), lambda b, i: (0, 0))],
        out_specs=pl.BlockSpec((1, tm, D), lambda b, i: (b, i, 0)),
        out_shape=jax.ShapeDtypeStruct((B, N, D), F32),
        compiler_params=_cparams(("parallel", "parallel")),
        name="pw1",
    )(x, sh, sc, g, w, b)


def _conv_kernel(x_ref, u_ref, uprev_ref, unext_ref, wdw_ref, bdw_ref, lng_ref, lnb_ref, w2_ref, b2_ref,
                 g1_ref, g2n_ref, sh2_ref, sc2_ref, rwa_ref, rwb_ref, xo_ref, h2_ref, aff_ref, ext_ref, cv_ref):
    i = pl.program_id(1)
    nt = pl.num_programs(1)
    tm = x_ref.shape[1]
    hl = CONV_HALO
    ext_ref[0:hl] = jnp.where(i > 0, uprev_ref[0], 0.0)
    ext_ref[hl:hl + tm] = u_ref[0]
    ext_ref[hl + tm:] = jnp.where(i < nt - 1, unext_ref[0], 0.0)
    lead = hl - CONV_WIDTH // 2

    def conv_rows(rb, carry):
        r0 = pl.multiple_of(rb * CONV_ROWS, CONV_ROWS)
        for lt in range(x_ref.shape[2] // LANES):
            cols = slice(lt * LANES, (lt + 1) * LANES)
            blk = ext_ref[pl.ds(r0, CONV_ROWS + 2 * hl), cols]
            acc = jnp.zeros((CONV_ROWS, LANES), F32)
            for r in range(8):
                view = pltpu.roll(blk, blk.shape[0] - (lead + r), 0)
                for a in range((CONV_WIDTH - r + 7) // 8):
                    k = 8 * a + r
                    acc = acc + view[8 * a:8 * a + CONV_ROWS] * wdw_ref[k:k + 1, cols]
            cv_ref[pl.ds(r0, CONV_ROWS), cols] = acc
        return carry

    lax.fori_loop(0, tm // CONV_ROWS, conv_rows, 0)
    acc = cv_ref[...] + bdw_ref[...]
    mu = jnp.mean(acc, axis=-1, keepdims=True)
    cen = acc - mu
    var = jnp.mean(cen * cen, axis=-1, keepdims=True)
    yn = cen * lax.rsqrt(var + EPS) * lng_ref[...] + lnb_ref[...]
    yn = yn * jax.nn.sigmoid(yn)
    y = _dot(yn.astype(BF16), w2_ref[...]) + b2_ref[...]
    xn = x_ref[0] + g1_ref[0] * y
    _post(xn, g2n_ref, sh2_ref, sc2_ref, rwa_ref, rwb_ref, xo_ref, h2_ref, aff_ref)


def _conv(x, u, wdw, bdw, lng, lnb, w2, b2, g1, g2n, sh2, sc2, rwa, rwb):
    B, N, D = x.shape
    tm = TOKEN_TILE
    hb = tm // CONV_HALO
    nhb = N // CONV_HALO
    vec = lambda: pl.BlockSpec((1, 1, D), lambda b, i: (b, 0, 0))
    row = lambda: pl.BlockSpec((1, D), lambda b, i: (0, 0))
    pin, pout = _post_specs(D, tm)
    return pl.pallas_call(
        _conv_kernel,
        grid=(B, N // tm),
        in_specs=[pl.BlockSpec((1, tm, D), lambda b, i: (b, i, 0)),
                  pl.BlockSpec((1, tm, D), lambda b, i: (b, i, 0)),
                  pl.BlockSpec((1, CONV_HALO, D), lambda b, i: (b, jnp.maximum(i * hb - 1, 0), 0)),
                  pl.BlockSpec((1, CONV_HALO, D), lambda b, i: (b, jnp.minimum((i + 1) * hb, nhb - 1), 0)),
                  pl.BlockSpec(wdw.shape, lambda b, i: (0, 0)),
                  row(), row(), row(),
                  pl.BlockSpec((D, D), lambda b, i: (0, 0)),
                  row(), vec()] + pin,
        out_specs=pout,
        out_shape=_post_shapes(B, N, D),
        scratch_shapes=[pltpu.VMEM((tm + 2 * CONV_HALO, D), F32), pltpu.VMEM((tm, D), F32)],
        compiler_params=_cparams(("parallel", "parallel")),
        name="conv",
    )(x, u, u, u, wdw, bdw, lng, lnb, w2, b2, g1, g2n, sh2, sc2, rwa, rwb)


def _rope_tables(n):
    rows = n // GRID_W
    row = jnp.repeat(jnp.arange(rows), GRID_W).astype(F32)
    col = jnp.tile(jnp.arange(GRID_W), rows).astype(F32)
    per_axis = QK_ROPE // 2
    inv_freq = 1.0 / (ROPE_THETA ** (jnp.arange(0, per_axis, 2, dtype=F32) / per_axis))
    ang = jnp.stack([row[:, None] * inv_freq, col[:, None] * inv_freq], axis=1)
    cos, sin = jnp.cos(ang), jnp.sin(ang)
    cos32 = jnp.broadcast_to(cos[:, :, None, :], (n, 2, 2, QK_ROPE // 4)).reshape(n, QK_ROPE)
    sin32 = jnp.stack([-sin, sin], axis=2).reshape(n, QK_ROPE)
    return cos32, sin32


def _head_tables(cos32, sin32, lead, scale):
    n = cos32.shape[0]
    cosf = jnp.concatenate([jnp.full((n, QK_NOPE), lead, F32), cos32, jnp.zeros((n, QK_ROPE), F32)], axis=1)
    sinf = jnp.concatenate([jnp.zeros((n, QK_NOPE), F32), sin32, jnp.zeros((n, QK_ROPE), F32)], axis=1)
    return cosf * scale, sinf * scale


def _layer0_weights(mix_w_in, w_uq, w_ukv):
    D = mix_w_in.shape[0]
    perm = jnp.arange(QK_ROPE) ^ (QK_ROPE // 4)
    rope_cols = mix_w_in[:, KV_OFF + KV_LORA:]
    wmix = jnp.concatenate([mix_w_in[:, :KV_OFF + KV_LORA], jnp.zeros((D, QK_NOPE), F32),
                            rope_cols, rope_cols[:, perm]], axis=1).astype(BF16)
    wq3 = w_uq.reshape(Q_LORA, MLA_HEADS, QK_DIM)
    wq = jnp.concatenate([wq3, wq3[:, :, QK_NOPE:][:, :, perm]], axis=2).reshape(Q_LORA, MLA_HEADS * HEAD_PAD)
    wkv3 = w_ukv.reshape(KV_LORA, MLA_HEADS, QK_NOPE + V_DIM)
    zpad = jnp.zeros((KV_LORA, MLA_HEADS, HEAD_PAD - QK_NOPE), F32)
    wk = jnp.concatenate([wkv3[:, :, :QK_NOPE], zpad], axis=2).reshape(KV_LORA, MLA_HEADS * HEAD_PAD)
    wvt = wkv3[:, :, QK_NOPE:].reshape(KV_LORA, MLA_HEADS * V_DIM).T
    return wmix, wq.astype(BF16), wk.astype(BF16), wvt.astype(BF16)


def _router_weights(rw):
    D, E = rw.shape
    rwp = jnp.concatenate([rw, jnp.zeros((D, LANES - E), F32)], axis=1)
    hi = rwp.astype(BF16)
    lo = (rwp - hi.astype(F32)).astype(BF16)
    return jnp.concatenate([hi, lo], axis=1), hi


def kernel(x, c, ctx, c_ctx, ada_w, ada_b, norm1_g, norm2_g, mix_w_in, pool_w, pool_scale, q_norm_g, kv_norm_g, w_uq, w_ukv, mix_w_out, conv_w_pw1, conv_b_pw1, conv_w_dw, conv_b_dw, conv_ln_g, conv_ln_b, conv_w_pw2, conv_b_pw2, router_w, exp_wg, exp_wu, exp_wd, final_g):
    B, N, D = x.shape
    LC = ctx.shape[1]

    cc = jnp.concatenate([c, c_ctx[None, :], jnp.zeros((16 - B - 1, D), F32)], axis=0)
    mods = _ada(cc, ada_w, ada_b)

    def mod(l, k):
        return mods[l, :B, k * D:(k + 1) * D].reshape(B, 1, D)

    def mod_ctx(l, k):
        return jnp.broadcast_to(mods[l, B, k * D:(k + 1) * D].reshape(1, 1, D), (B, 1, D))

    row = lambda v: v.reshape(1, -1)

    wmix, wq, wk, wvt = _layer0_weights(mix_w_in[0], w_uq[0], w_ukv[0])
    qscale = (QK_DIM ** -0.5) * math.log2(math.e)
    cos32, sin32 = _rope_tables(N)
    cq, sq = _head_tables(cos32, sin32, 1.0, qscale)
    ck, sk = _head_tables(cos32, sin32, 0.0, 1.0)
    one32, zero32 = jnp.ones((LC, QK_ROPE), F32), jnp.zeros((LC, QK_ROPE), F32)
    cqc, sqc = _head_tables(one32, zero32, 1.0, qscale)
    ckc, skc = _head_tables(one32, zero32, 0.0, 1.0)
    lw = (row(norm1_g[0]), wmix, row(q_norm_g[0]), wq, row(kv_norm_g[0]), wk, wvt)
    q, k, vt, zp = _mixin(x, mod(0, 0), mod(0, 1), *lw, cq, sq, ck, sk, TOKEN_TILE)
    _, kc, vtc, _ = _mixin(ctx, mod_ctx(0, 0), mod_ctx(0, 1), *lw, cqc, sqc, ckc, skc, LC)
    attn = _attn(q, k, vt, kc, vtc)
    rwa, rwb = _router_weights(router_w[0])
    x, h2, aff = _mixout(x, zp, attn, pool_w[0].astype(BF16), row(pool_scale[0]), mix_w_out[0].astype(BF16),
                         mod(0, 2), row(norm2_g[0]), mod(0, 3), mod(0, 4), rwa, rwb)
    x = _moe(x, h2, aff, mod(0, 5), exp_wg, exp_wu, exp_wd, 0, row(final_g), False)

    u = _pw1(x, mod(1, 0), mod(1, 1), row(norm1_g[1]), conv_w_pw1[0].astype(BF16), row(conv_b_pw1[0]))
    wdw = jnp.concatenate([conv_w_dw[0], jnp.zeros((1, D), F32)], axis=0)
    rwa, rwb = _router_weights(router_w[1])
    x, h2, aff = _conv(x, u, wdw, row(conv_b_dw[0]), row(conv_ln_g[0]), row(conv_ln_b[0]),
                       conv_w_pw2[0].astype(BF16), row(conv_b_pw2[0]),
                       mod(1, 2), row(norm2_g[1]), mod(1, 3), mod(1, 4), rwa, rwb)
    return _moe(x, h2, aff, mod(1, 5), exp_wg, exp_wu, exp_wd, 1, row(final_g), True)
```

```python
import functools
import math

import jax
import jax.numpy as jnp
from jax import lax
from jax.experimental import pallas as pl
from jax.experimental.pallas import tpu as pltpu

F32 = jnp.float32
BF16 = jnp.bfloat16

D_MODEL = 1024
BATCH = 8
SEQ = 4096
CTX_LEN = 256
GRID_W = 64
POOL_WINDOWS = (2, 4, 8, 16)
POOL_GROUP_DIM = 128
POOL_DIM = 512
MLA_HEADS = 8
QK_NOPE = 64
QK_ROPE = 32
V_DIM = 64
Q_LORA = 256
KV_LORA = 128
QK_DIM = QK_NOPE + QK_ROPE
KV_OFF = POOL_DIM + Q_LORA
ROPE_THETA = 10000.0
CONV_WIDTH = 31
N_EXPERTS = 16
CAP = 2 * SEQ // N_EXPERTS
EPS = 1e-6

LANES = 128
HEAD_PAD = 128
POOL_HALO = 8
CONV_HALO = 16
CONV_ROWS = 64
TOKEN_TILE = 512
Q_TILE = 512
ATTN_KEY_CHUNK = 512
MOE_BLOCK = 256
SLOT_WIN = 64
SLOT_ALIGN = 16
WIN_OVERFLOW = 4096.0
VMEM_LIMIT = 52 * 1024 * 1024


def _cparams(sem):
    return pltpu.CompilerParams(dimension_semantics=sem, vmem_limit_bytes=VMEM_LIMIT)


def _rms(xf, g):
    ms = jnp.mean(xf * xf, axis=-1, keepdims=True)
    return xf * lax.rsqrt(ms + EPS) * g


def _dot(a, b):
    return jnp.dot(a, b, preferred_element_type=F32)


def _dot_nt(a, b):
    return lax.dot_general(a, b, (((1,), (1,)), ((), ())), preferred_element_type=F32)


def _ada_kernel(c_ref, w_ref, b_ref, o_ref):
    cv = c_ref[...]
    s = cv * jax.nn.sigmoid(cv)
    o_ref[0] = jnp.dot(s, w_ref[0], precision=lax.Precision.HIGHEST,
                       preferred_element_type=F32) + b_ref[0]


def _ada(cc, ada_w, ada_b):
    L, D, D6 = ada_w.shape
    tn = 1536
    return pl.pallas_call(
        _ada_kernel,
        grid=(L, D6 // tn),
        in_specs=[pl.BlockSpec((16, D), lambda l, j: (0, 0)),
                  pl.BlockSpec((1, D, tn), lambda l, j: (l, 0, j)),
                  pl.BlockSpec((1, 1, tn), lambda l, j: (l, 0, j))],
        out_specs=pl.BlockSpec((1, 16, tn), lambda l, j: (l, 0, j)),
        out_shape=jax.ShapeDtypeStruct((L, 16, D6), F32),
        compiler_params=_cparams(("parallel", "parallel")),
        name="ada",
    )(cc, ada_w, ada_b.reshape(L, 1, D6))


def _mixin_kernel(x_ref, sh_ref, sc_ref, g_ref, wmix_ref, qg_ref, wq_ref, kvg_ref, wk_ref, wvt_ref,
                  cq_ref, sq_ref, ck_ref, sk_ref, q_ref, k_ref, vt_ref, zp_ref):
    h = _rms(x_ref[0], g_ref[...]) * (1.0 + sc_ref[0]) + sh_ref[0]
    z = _dot(h.astype(BF16), wmix_ref[...])
    zp_ref[0] = z[:, :POOL_DIM]
    zq = _rms(z[:, POOL_DIM:KV_OFF], qg_ref[...]).astype(BF16)
    qa = _dot(zq, wq_ref[...])
    cq = cq_ref[...]
    sq = sq_ref[...]
    for hh in range(MLA_HEADS):
        qh = qa[:, hh * HEAD_PAD:(hh + 1) * HEAD_PAD]
        q_ref[0, hh] = (qh * cq + pltpu.roll(qh, HEAD_PAD - QK_ROPE, 1) * sq).astype(BF16)
    zkv = _rms(z[:, KV_OFF:KV_OFF + KV_LORA], kvg_ref[...]).astype(BF16)
    ka = _dot(zkv, wk_ref[...])
    zr = z[:, KV_OFF + KV_LORA:]
    kr = zr * ck_ref[...] + pltpu.roll(zr, HEAD_PAD - QK_ROPE, 1) * sk_ref[...]
    for hh in range(MLA_HEADS):
        k_ref[0, hh] = (ka[:, hh * HEAD_PAD:(hh + 1) * HEAD_PAD] + kr).astype(BF16)
    vt_ref[0] = _dot_nt(wvt_ref[...], zkv).astype(BF16)


def _mixin(x, sh, sc, g, wmix, qg, wq, kvg, wk, wvt, cq, sq, ck, sk, tm):
    B, N, D = x.shape
    H = MLA_HEADS
    vec = lambda: pl.BlockSpec((1, 1, D), lambda b, i: (b, 0, 0))
    full = lambda a: pl.BlockSpec(a.shape, lambda b, i: (0,) * a.ndim)
    tab = lambda: pl.BlockSpec((tm, LANES), lambda b, i: (i, 0))
    hd = lambda: pl.BlockSpec((1, H, tm, HEAD_PAD), lambda b, i: (b, 0, i, 0))
    return pl.pallas_call(
        _mixin_kernel,
        grid=(B, N // tm),
        in_specs=[pl.BlockSpec((1, tm, D), lambda b, i: (b, i, 0)), vec(), vec(), full(g), full(wmix),
                  full(qg), full(wq), full(kvg), full(wk), full(wvt), tab(), tab(), tab(), tab()],
        out_specs=[hd(), hd(), pl.BlockSpec((1, H * V_DIM, tm), lambda b, i: (b, 0, i)),
                   pl.BlockSpec((1, tm, POOL_DIM), lambda b, i: (b, i, 0))],
        out_shape=[jax.ShapeDtypeStruct((B, H, N, HEAD_PAD), BF16)] * 2
        + [jax.ShapeDtypeStruct((B, H * V_DIM, N), BF16), jax.ShapeDtypeStruct((B, N, POOL_DIM), F32)],
        compiler_params=_cparams(("parallel", "parallel")),
        name="mixin",
    )(x, sh, sc, g, wmix, qg, wq, kvg, wk, wvt, cq, sq, ck, sk)


def _col_reduce(x, op):
    for group in (256, 64, 8):
        if x.shape[0] > group:
            x = op(x.reshape(x.shape[0] // group, group, x.shape[1]), axis=0)
    return op(x, axis=0, keepdims=True)


def _attn_kernel(q_ref, qn_ref, k_ref, vt_ref, kc_ref, vtc_ref, o_ref, s0_ref, s1_ref, m0_ref):
    tq = q_ref.shape[2]
    lc = kc_ref.shape[2]
    ck = ATTN_KEY_CHUNK
    nchunk = 1 + k_ref.shape[2] // ck

    def key_rows(c):
        return slice(0, lc) if c == 0 else slice(lc + (c - 1) * ck, lc + c * ck)

    def fill(dst_ref, qr, j, c, m):
        keys = kc_ref[0, j] if c == 0 else k_ref[0, j, (c - 1) * ck:c * ck, :]
        s = _dot_nt(keys, qr[0, j])
        dst_ref[key_rows(c), :] = s
        r = _col_reduce(s, jnp.max)
        return r if m is None else jnp.maximum(m, r)

    def consume(src_ref, j, c, m, acc, l):
        rows = slice(j * V_DIM, (j + 1) * V_DIM)
        vals = vtc_ref[0, rows, :] if c == 0 else vt_ref[0, rows, (c - 1) * ck:c * ck]
        p = jnp.exp2(src_ref[key_rows(c), :] - m)
        return acc + _dot(vals, p.astype(BF16)), l + _col_reduce(p, jnp.sum)

    @pl.when(pl.program_id(2) == 0)
    def _():
        m = None
        for c in range(nchunk):
            m = fill(s0_ref, q_ref, 0, c, m)
        m0_ref[...] = m

    zero = (jnp.zeros((V_DIM, tq), F32), jnp.zeros((1, tq), F32))
    m0 = m0_ref[...]
    m1 = None
    acc0, l0 = zero
    for c in range(nchunk):
        m1 = fill(s1_ref, q_ref, 1, c, m1)
        acc0, l0 = consume(s0_ref, 0, c, m0, acc0, l0)
    mn = None
    acc1, l1 = zero
    for c in range(nchunk):
        mn = fill(s0_ref, qn_ref, 0, c, mn)
        acc1, l1 = consume(s1_ref, 1, c, m1, acc1, l1)
    m0_ref[...] = mn
    o_ref[0] = jnp.concatenate([acc0 / l0, acc1 / l1], axis=0).T.astype(BF16)


def _attn(q, k, vt, kc, vtc):
    B, H, N, P = q.shape
    LC = kc.shape[2]
    tq = Q_TILE
    nq = N // tq
    kspec = lambda n: pl.BlockSpec((1, 2, n, P), lambda b, p, i: (b, p, 0, 0))
    vspec = lambda n: pl.BlockSpec((1, 2 * V_DIM, n), lambda b, p, i: (b, p, 0))
    return pl.pallas_call(
        _attn_kernel,
        grid=(B, H // 2, nq),
        in_specs=[pl.BlockSpec((1, 2, tq, P), lambda b, p, i: (b, p, i, 0)),
                  pl.BlockSpec((1, 2, tq, P), lambda b, p, i: (b, p, jnp.minimum(i + 1, nq - 1), 0)),
                  kspec(N), vspec(N), kspec(LC), vspec(LC)],
        out_specs=pl.BlockSpec((1, tq, LANES), lambda b, p, i: (b, i, p)),
        out_shape=jax.ShapeDtypeStruct((B, N, MLA_HEADS * V_DIM), BF16),
        scratch_shapes=[pltpu.VMEM((LC + N, tq), F32), pltpu.VMEM((LC + N, tq), F32), pltpu.VMEM((1, tq), F32)],
        compiler_params=_cparams(("parallel", "parallel", "arbitrary")),
        name="attn",
    )(q, q, k, vt, kc, vtc)


def _post(xn, g2n_ref, sh2_ref, sc2_ref, rwa_ref, rwb_ref, xo_ref, h2_ref, aff_ref):
    xo_ref[0] = xn
    h2 = _rms(xn, g2n_ref[...]) * (1.0 + sc2_ref[0]) + sh2_ref[0]
    hi = h2.astype(BF16)
    lo = (h2 - hi.astype(F32)).astype(BF16)
    h2_ref[0] = hi
    la = _dot(hi, rwa_ref[...])
    lb = _dot(lo, rwb_ref[...])
    logits = la[:, :LANES] + la[:, LANES:] + lb
    lane = lax.broadcasted_iota(jnp.int32, logits.shape, 1)
    logits = jnp.where(lane < N_EXPERTS, logits, -1e30)
    m = jnp.max(logits, axis=-1, keepdims=True)
    e = jnp.exp(logits - m)
    aff_ref[0] = e / jnp.sum(e, axis=-1, keepdims=True)


def _post_specs(D, tm):
    vec = lambda: pl.BlockSpec((1, 1, D), lambda b, i: (b, 0, 0))
    in_specs = [pl.BlockSpec((1, D), lambda b, i: (0, 0)), vec(), vec(),
                pl.BlockSpec((D, 2 * LANES), lambda b, i: (0, 0)),
                pl.BlockSpec((D, LANES), lambda b, i: (0, 0))]
    out_specs = [pl.BlockSpec((1, tm, D), lambda b, i: (b, i, 0)),
                 pl.BlockSpec((1, tm, D), lambda b, i: (b, i, 0)),
                 pl.BlockSpec((1, tm, LANES), lambda b, i: (b, i, 0))]
    return in_specs, out_specs


def _post_shapes(B, N, D):
    return [jax.ShapeDtypeStruct((B, N, D), F32), jax.ShapeDtypeStruct((B, N, D), BF16),
            jax.ShapeDtypeStruct((B, N, LANES), F32)]


def _mixout_kernel(x_ref, zp_ref, zprev_ref, znext_ref, attn_ref, pw_ref, ps_ref, wout_ref, g1_ref,
                   g2n_ref, sh2_ref, sc2_ref, rwa_ref, rwb_ref, xo_ref, h2_ref, aff_ref, ext_ref):
    i = pl.program_id(1)
    nt = pl.num_programs(1)
    tm = x_ref.shape[1]
    hl = POOL_HALO
    ext_ref[0:hl] = jnp.where(i > 0, zprev_ref[0], 0.0)
    ext_ref[hl:hl + tm] = zp_ref[0]
    ext_ref[hl + tm:] = jnp.where(i < nt - 1, znext_ref[0], 0.0)
    t = i * tm + lax.broadcasted_iota(jnp.int32, (tm, 1), 0)
    ys = []
    for g, w in enumerate(POOL_WINDOWS):
        cols = slice(g * POOL_GROUP_DIM, (g + 1) * POOL_GROUP_DIM)
        s = ext_ref[:, cols]
        n = s.shape[0]
        span = 1
        while span < w:
            s = s + pltpu.roll(s, n - span, 0)
            span *= 2
        first = hl - w // 2
        s = (pltpu.roll(s, n - first, 0) if first else s)[:tm]
        lo = jnp.maximum(t - w // 2, 0)
        hi = jnp.minimum(t - w // 2 + w, SEQ)
        d = s / (hi - lo).astype(F32) - ext_ref[hl:hl + tm, cols]
        ys.append(_dot(d.astype(BF16), pw_ref[g]))
    pool = (jnp.concatenate(ys, axis=-1) * ps_ref[...]).astype(BF16)
    y = _dot(pool, wout_ref[0:POOL_DIM]) + _dot(attn_ref[0], wout_ref[POOL_DIM:])
    xn = x_ref[0] + g1_ref[0] * y
    _post(xn, g2n_ref, sh2_ref, sc2_ref, rwa_ref, rwb_ref, xo_ref, h2_ref, aff_ref)


def _mixout(x, zp, attn, pw, ps, wout, g1, g2n, sh2, sc2, rwa, rwb):
    B, N, D = x.shape
    tm = TOKEN_TILE
    hb = tm // POOL_HALO
    nhb = N // POOL_HALO
    vec = lambda: pl.BlockSpec((1, 1, D), lambda b, i: (b, 0, 0))
    pin, pout = _post_specs(D, tm)
    return pl.pallas_call(
        _mixout_kernel,
        grid=(B, N // tm),
        in_specs=[pl.BlockSpec((1, tm, D), lambda b, i: (b, i, 0)),
                  pl.BlockSpec((1, tm, POOL_DIM), lambda b, i: (b, i, 0)),
                  pl.BlockSpec((1, POOL_HALO, POOL_DIM), lambda b, i: (b, jnp.maximum(i * hb - 1, 0), 0)),
                  pl.BlockSpec((1, POOL_HALO, POOL_DIM),
                               lambda b, i: (b, jnp.minimum((i + 1) * hb, nhb - 1), 0)),
                  pl.BlockSpec((1, tm, POOL_DIM), lambda b, i: (b, i, 0)),
                  pl.BlockSpec(pw.shape, lambda b, i: (0, 0, 0)),
                  pl.BlockSpec((1, POOL_DIM), lambda b, i: (0, 0)),
                  pl.BlockSpec(wout.shape, lambda b, i: (0, 0)),
                  vec()] + pin,
        out_specs=pout,
        out_shape=_post_shapes(B, N, D),
        scratch_shapes=[pltpu.VMEM((tm + 2 * POOL_HALO, POOL_DIM), F32)],
        compiler_params=_cparams(("parallel", "parallel")),
        name="mixout",
    )(x, zp, zp, zp, attn, pw, ps, wout, g1, g2n, sh2, sc2, rwa, rwb)


def _topk_kernel(aff_ref, rank_ref, rank_t_ref, gp_ref, wa_ref, wc_ref):
    n = aff_ref.shape[1]
    nchunk = n // LANES

    def search(it, thr):
        bits = pltpu.bitcast(aff_ref[0], jnp.int32)
        cand = thr | (jnp.int32(1) << (30 - it))
        cnt = _col_reduce(jnp.where(bits >= cand, 1.0, 0.0), jnp.sum)
        return jnp.where(cnt >= CAP, cand, thr)

    thr = lax.fori_loop(0, 31, search, jnp.zeros((1, LANES), jnp.int32))

    ri = lax.broadcasted_iota(jnp.int32, (LANES, LANES), 0)
    ci = lax.broadcasted_iota(jnp.int32, (LANES, LANES), 1)
    ltri = jnp.where(ri > ci, 1.0, 0.0).astype(BF16)

    def prefix(mask):
        run = jnp.zeros((1, LANES), F32)
        outs = []
        for c in range(nchunk):
            mc = mask[c * LANES:(c + 1) * LANES]
            outs.append(_dot(ltri, mc.astype(BF16)) + run)
            run = run + _col_reduce(mc, jnp.sum)
        return jnp.concatenate(outs, axis=0), run

    a = aff_ref[0]
    bits = pltpu.bitcast(a, jnp.int32)
    gt = jnp.where(bits > thr, 1.0, 0.0)
    eq = jnp.where(bits == thr, 1.0, 0.0)
    need = CAP - _col_reduce(gt, jnp.sum)
    pe, _ = prefix(eq)
    sel = gt + eq * jnp.where(pe < need, 1.0, 0.0)
    pref, total = prefix(sel)
    rank = jnp.where(sel > 0.0, pref, -1.0)
    rank_ref[0] = rank
    for c in range(nchunk):
        rank_t_ref[0, :, c * LANES:(c + 1) * LANES] = rank[c * LANES:(c + 1) * LANES].T[:N_EXPERTS]
    gate = jnp.where(sel > 0.0, a, 0.0)
    g_hi = gate.astype(BF16)
    r1 = gate - g_hi.astype(F32)
    g_mid = r1.astype(BF16)
    g_lo = (r1 - g_mid.astype(F32)).astype(BF16)
    gp_ref[0] = jnp.concatenate([g_hi, g_mid, g_lo], axis=-1)
    lo =[pref[j * MOE_BLOCK:j * MOE_BLOCK + 1] for j in range(n // MOE_BLOCK)] + [total]
    wa, wc = [], []
    for j in range(n // MOE_BLOCK):
        start = jnp.minimum(jnp.floor(lo[j] * (1.0 / SLOT_ALIGN)) * SLOT_ALIGN, float(CAP - SLOT_WIN))
        wa.append(start)
        wc.append(start + jnp.where(lo[j + 1] > start + SLOT_WIN, WIN_OVERFLOW, 0.0))
    wa_ref[0] = jnp.concatenate(wa, axis=0)
    wc_ref[0] = jnp.concatenate(wc, axis=0)


def _topk(aff):
    B, N, E = aff.shape
    nj = N // MOE_BLOCK
    spec = lambda: pl.BlockSpec((1, N, E), lambda b: (b, 0, 0))
    wspec = lambda: pl.BlockSpec((1, nj, E), lambda b: (b, 0, 0))
    return pl.pallas_call(
        _topk_kernel,
        grid=(B,),
        in_specs=[spec()],
        out_specs=[spec(), pl.BlockSpec((1, N_EXPERTS, N), lambda b: (b, 0, 0)),
                   pl.BlockSpec((1, N, 3 * E), lambda b: (b, 0, 0)), wspec(), wspec()],
        out_shape=[jax.ShapeDtypeStruct((B, N, E), F32), jax.ShapeDtypeStruct((B, N_EXPERTS, N), F32),
                   jax.ShapeDtypeStruct((B, N, 3 * E), BF16),
                   jax.ShapeDtypeStruct((B, nj, E), F32), jax.ShapeDtypeStruct((B, nj, E), F32)],
        compiler_params=_cparams(("parallel",)),
        name="topk",
    )(aff)


def _gather_kernel(wa_s, ovf_s, rank_t_ref, h2_ref, gp_ref, xs_ref, gs_ref):
    b = pl.program_id(0)
    j = pl.program_id(1)
    base = (b * pl.num_programs(1) + j) * N_EXPERTS
    tb = h2_ref.shape[1]

    @pl.when(j == 0)
    def _():
        xs_ref[...] = jnp.zeros_like(xs_ref)
        gs_ref[...] = jnp.zeros_like(gs_ref)

    def gate_sum(g):
        return g[:, :LANES] + g[:, LANES:2 * LANES] + g[:, 2 * LANES:]

    sub = lax.broadcasted_iota(jnp.int32, (SLOT_WIN, tb), 0).astype(F32)
    ps = []
    for e in range(N_EXPERTS):
        wcmp = (wa_s[base + e] + ovf_s[base + e] * int(WIN_OVERFLOW)).astype(F32)
        ps.append(jnp.where(rank_t_ref[0, e:e + 1, :] - wcmp == sub, 1.0, 0.0).astype(BF16))
    pcat = jnp.concatenate(ps, axis=0)
    res = _dot(pcat, h2_ref[0])
    gres = gate_sum(_dot(pcat, gp_ref[0]))
    for e in range(N_EXPERTS):
        wa = pl.multiple_of(wa_s[base + e], SLOT_ALIGN)
        rows = slice(e * SLOT_WIN, (e + 1) * SLOT_WIN)
        xs_ref[0, e, pl.ds(wa, SLOT_WIN), :] += res[rows].astype(BF16)
        gs_ref[0, e, pl.ds(wa, SLOT_WIN), :] += gres[rows]

    def overflow(e, carry):
        @pl.when(ovf_s[base + e] != 0)
        def _():
            r = rank_t_ref[0, pl.ds(e, 1), :]
            for k in range(CAP // SLOT_WIN):
                p = jnp.where(r - float(k * SLOT_WIN) == sub, 1.0, 0.0).astype(BF16)
                rows = slice(k * SLOT_WIN, (k + 1) * SLOT_WIN)
                xs_ref[0, e, rows, :] += _dot(p, h2_ref[0]).astype(BF16)
                gs_ref[0, e, rows, :] += gate_sum(_dot(p, gp_ref[0]))
        return carry

    lax.fori_loop(0, N_EXPERTS, overflow, 0)


def _gather(wa_s, ovf_s, rank_t, h2, gp):
    B, N, D = h2.shape
    E = N_EXPERTS
    tb = MOE_BLOCK
    return pl.pallas_call(
        _gather_kernel,
        grid_spec=pltpu.PrefetchScalarGridSpec(
            num_scalar_prefetch=2,
            grid=(B, N // tb),
            in_specs=[pl.BlockSpec((1, E, tb), lambda b, j, *_: (b, 0, j)),
                      pl.BlockSpec((1, tb, D), lambda b, j, *_: (b, j, 0)),
                      pl.BlockSpec((1, tb, 3 * LANES), lambda b, j, *_: (b, j, 0))],
            out_specs=[pl.BlockSpec((1, E, CAP, D), lambda b, j, *_: (b, 0, 0, 0)),
                       pl.BlockSpec((1, E, CAP, LANES), lambda b, j, *_: (b, 0, 0, 0))]),
        out_shape=[jax.ShapeDtypeStruct((B, E, CAP, D), BF16), jax.ShapeDtypeStruct((B, E, CAP, LANES), F32)],
        compiler_params=_cparams(("parallel", "arbitrary")),
        name="gather",
    )(wa_s, ovf_s, rank_t, h2, gp)


def _ffn_kernel(xs_ref, gs_ref, wg_ref, wu_ref, wd_ref, y_ref, wgb_ref, wub_ref, wdb_ref):
    e = pl.program_id(0)

    @pl.when(pl.program_id(1) == 0)
    def _():
        wgb_ref[...] = wg_ref[0, 0].astype(BF16)
        wub_ref[...] = wu_ref[0, 0].astype(BF16)
        wdb_ref[...] = wd_ref[0, 0].astype(BF16)

    xs = xs_ref[0, 0]
    a = _dot(xs, wgb_ref[...])
    u = _dot(xs, wub_ref[...])
    hm = (a * jax.nn.sigmoid(a) * u).astype(BF16)
    lane = lax.broadcasted_iota(jnp.int32, gs_ref.shape[2:], 1)
    gate = jnp.sum(jnp.where(lane == e, gs_ref[0, 0], 0.0), axis=-1, keepdims=True)
    y_ref[0, 0] = (_dot(hm, wdb_ref[...]) * gate).astype(BF16)


def _ffn(xs, gs, wg, wu, wd, layer):
    B, E, C, D = xs.shape
    F = wg.shape[3]
    return pl.pallas_call(
        _ffn_kernel,
        grid=(E, B),
        in_specs=[pl.BlockSpec((1, 1, C, D), lambda e, b: (b, e, 0, 0)),
                  pl.BlockSpec((1, 1, C, LANES), lambda e, b: (b, e, 0, 0)),
                  pl.BlockSpec((1, 1, D, F), lambda e, b: (layer, e, 0, 0)),
                  pl.BlockSpec((1, 1, D, F), lambda e, b: (layer, e, 0, 0)),
                  pl.BlockSpec((1, 1, F, D), lambda e, b: (layer, e, 0, 0))],
        out_specs=pl.BlockSpec((1, 1, C, D), lambda e, b: (b, e, 0, 0)),
        out_shape=jax.ShapeDtypeStruct((B, E, C, D), BF16),
        scratch_shapes=[pltpu.VMEM((D, F), BF16), pltpu.VMEM((D, F), BF16), pltpu.VMEM((F, D), BF16)],
        compiler_params=_cparams(("arbitrary", "arbitrary")),
        name="ffn",
    )(xs, gs, wg, wu, wd)


def _combine_kernel(wa_s, ovf_s, y_ref, rank_ref, wc_ref, spread_ref, x_ref, g2_ref, fg_ref, o_ref,
                    ycat_ref, acc_ref, *, final):
    b = pl.program_id(0)
    j = pl.program_id(1)
    base = (b * pl.num_programs(1) + j) * N_EXPERTS
    tb = x_ref.shape[1]
    kc = N_EXPERTS * SLOT_WIN

    for e in range(N_EXPERTS):
        wa = pl.multiple_of(wa_s[base + e], SLOT_ALIGN)
        ycat_ref[e * SLOT_WIN:(e + 1) * SLOT_WIN, :] = y_ref[0, e, pl.ds(wa, SLOT_WIN), :]
    rel = rank_ref[0] - wc_ref[0, pl.ds(j, 1), :]
    rel = jnp.where(rel >= 0.0, jnp.where(rel < float(SLOT_WIN), rel, float(SLOT_WIN)), float(SLOT_WIN))
    spread = _dot(rel.astype(BF16), spread_ref[...])
    col = (lax.broadcasted_iota(jnp.int32, (tb, kc), 1) % SLOT_WIN).astype(F32)
    pt = jnp.where(spread == col, 1.0, 0.0).astype(BF16)
    acc_ref[...] = _dot(pt, ycat_ref[...])

    def overflow(e, carry):
        @pl.when(ovf_s[base + e] != 0)
        def _():
            lane = lax.broadcasted_iota(jnp.int32, (tb, LANES), 1)
            rk = jnp.sum(jnp.where(lane == e, rank_ref[0], 0.0), axis=-1, keepdims=True)
            slot = lax.broadcasted_iota(jnp.int32, (tb, SLOT_WIN), 1).astype(F32)
            for k in range(CAP // SLOT_WIN):
                p = jnp.where(rk - float(k * SLOT_WIN) == slot, 1.0, 0.0).astype(BF16)
                acc_ref[...] += _dot(p, y_ref[0, e, k * SLOT_WIN:(k + 1) * SLOT_WIN, :])
        return carry

    lax.fori_loop(0, N_EXPERTS, overflow, 0)

    xn = x_ref[0] + g2_ref[0] * acc_ref[...]
    if final:
        xn = _rms(xn, fg_ref[...])
    o_ref[0] = xn


def _combine(wa_s, ovf_s, y, rank, wc, x, g2, fg, final):
    B, N, D = x.shape
    E = N_EXPERTS
    tb = MOE_BLOCK
    kc = E * SLOT_WIN
    spread = (jnp.arange(LANES)[:, None] == jnp.arange(kc)[None, :] // SLOT_WIN).astype(BF16)
    return pl.pallas_call(
        functools.partial(_combine_kernel, final=final),
        grid_spec=pltpu.PrefetchScalarGridSpec(
            num_scalar_prefetch=2,
            grid=(B, N // tb),
            in_specs=[pl.BlockSpec((1, E, CAP, D), lambda b, j, *_: (b, 0, 0, 0)),
                      pl.BlockSpec((1, tb, LANES), lambda b, j, *_: (b, j, 0)),
                      pl.BlockSpec((1, N // tb, LANES), lambda b, j, *_: (b, 0, 0)),
                      pl.BlockSpec((LANES, kc), lambda b, j, *_: (0, 0)),
                      pl.BlockSpec((1, tb, D), lambda b, j, *_: (b, j, 0)),
                      pl.BlockSpec((1, 1, D), lambda b, j, *_: (b, 0, 0)),
                      pl.BlockSpec((1, D), lambda b, j, *_: (0, 0))],
            out_specs=pl.BlockSpec((1, tb, D), lambda b, j, *_: (b, j, 0)),
            scratch_shapes=[pltpu.VMEM((kc, D), BF16), pltpu.VMEM((tb, D), F32)]),
        out_shape=jax.ShapeDtypeStruct((B, N, D), F32),
        compiler_params=_cparams(("parallel", "arbitrary")),
        name="combine",
    )(wa_s, ovf_s, y, rank, wc, spread, x, g2, fg)


def _moe(x, h2, aff, g2, wg, wu, wd, layer, fg, final):
    rank, rank_t, gp, wa, wc = _topk(aff)
    wa_s = wa[:, :, :N_EXPERTS].astype(jnp.int32).reshape(-1)
    ovf_s = (wc[:, :, :N_EXPERTS] != wa[:, :, :N_EXPERTS]).astype(jnp.int32).reshape(-1)
    xs, gs = _gather(wa_s, ovf_s, rank_t, h2, gp)
    y = _ffn(xs, gs, wg, wu, wd, layer)
    return _combine(wa_s, ovf_s, y, rank, wc, x, g2, fg, final)


def _pw1_kernel(x_ref, sh_ref, sc_ref, g_ref, w_ref, b_ref, u_ref):
    d = x_ref.shape[2]
    h = _rms(x_ref[0], g_ref[...]) * (1.0 + sc_ref[0]) + sh_ref[0]
    z = _dot(h.astype(BF16), w_ref[...]) + b_ref[...]
    u_ref[0] = z[:, :d] * jax.nn.sigmoid(z[:, d:])


def _pw1(x, sh, sc, g, w, b):
    B, N, D = x.shape
    tm = TOKEN_TILE
    vec = lambda: pl.BlockSpec((1, 1, D), lambda b, i: (b, 0, 0))
    return pl.pallas_call(
        _pw1_kernel,
        grid=(B, N // tm),
        in_specs=[pl.BlockSpec((1, tm, D), lambda b, i: (b, i, 0)), vec(), vec(),
                  pl.BlockSpec((1, D), lambda b, i: (0, 0)),
                  pl.BlockSpec((D, 2 * D), lambda b, i: (0, 0)),
                  pl.BlockSpec((1, 2 * D), lambda b, i: (0, 0))],
        out_specs=pl.BlockSpec((1, tm, D), lambda b, i: (b, i, 0)),
        out_shape=jax.ShapeDtypeStruct((B, N, D), F32),
        compiler_params=_cparams(("parallel", "parallel")),
        name="pw1",
    )(x, sh, sc, g, w, b)


def _conv_kernel(x_ref, u_ref, uprev_ref, unext_ref, wdw_ref, bdw_ref, lng_ref, lnb_ref, w2_ref, b2_ref,
                 g1_ref, g2n_ref, sh2_ref, sc2_ref, rwa_ref, rwb_ref, xo_ref, h2_ref, aff_ref, ext_ref, cv_ref):
    i = pl.program_id(1)
    nt = pl.num_programs(1)
    tm = x_ref.shape[1]
    hl = CONV_HALO
    ext_ref[0:hl] = jnp.where(i > 0, uprev_ref[0], 0.0)
    ext_ref[hl:hl + tm] = u_ref[0]
    ext_ref[hl + tm:] = jnp.where(i < nt - 1, unext_ref[0], 0.0)
    lead = hl - CONV_WIDTH // 2

    def conv_rows(rb, carry):
        r0 = pl.multiple_of(rb * CONV_ROWS, CONV_ROWS)
        for lt in range(x_ref.shape[2] // LANES):
            cols = slice(lt * LANES, (lt + 1) * LANES)
            blk = ext_ref[pl.ds(r0, CONV_ROWS + 2 * hl), cols]
            acc = jnp.zeros((CONV_ROWS, LANES), F32)
            for r in range(8):
                view = pltpu.roll(blk, blk.shape[0] - (lead + r), 0)
                for a in range((CONV_WIDTH - r + 7) // 8):
                    k = 8 * a + r
                    acc = acc + view[8 * a:8 * a + CONV_ROWS] * wdw_ref[k:k + 1, cols]
            cv_ref[pl.ds(r0, CONV_ROWS), cols] = acc
        return carry

    lax.fori_loop(0, tm // CONV_ROWS, conv_rows, 0)
    acc = cv_ref[...] + bdw_ref[...]
    mu = jnp.mean(acc, axis=-1, keepdims=True)
    cen = acc - mu
    var = jnp.mean(cen * cen, axis=-1, keepdims=True)
    yn = cen * lax.rsqrt(var + EPS) * lng_ref[...] + lnb_ref[...]
    yn = yn * jax.nn.sigmoid(yn)
    y = _dot(yn.astype(BF16), w2_ref[...]) + b2_ref[...]
    xn = x_ref[0] + g1_ref[0] * y
    _post(xn, g2n_ref, sh2_ref, sc2_ref, rwa_ref, rwb_ref, xo_ref, h2_ref, aff_ref)


def _conv(x, u, wdw, bdw, lng, lnb, w2, b2, g1, g2n, sh2, sc2, rwa, rwb):
    B, N, D = x.shape
    tm = TOKEN_TILE
    hb = tm // CONV_HALO
    nhb = N // CONV_HALO
    vec = lambda: pl.BlockSpec((1, 1, D), lambda b, i: (b, 0, 0))
    row = lambda: pl.BlockSpec((1, D), lambda b, i: (0, 0))
    pin, pout = _post_specs(D, tm)
    return pl.pallas_call(
        _conv_kernel,
        grid=(B, N // tm),
        in_specs=[pl.BlockSpec((1, tm, D), lambda b, i: (b, i, 0)),
                  pl.BlockSpec((1, tm, D), lambda b, i: (b, i, 0)),
                  pl.BlockSpec((1, CONV_HALO, D), lambda b, i: (b, jnp.maximum(i * hb - 1, 0), 0)),
                  pl.BlockSpec((1, CONV_HALO, D), lambda b, i: (b, jnp.minimum((i + 1) * hb, nhb - 1), 0)),
                  pl.BlockSpec(wdw.shape, lambda b, i: (0, 0)),
                  row(), row(), row(),
                  pl.BlockSpec((D, D), lambda b, i: (0, 0)),
                  row(), vec()] + pin,
        out_specs=pout,
        out_shape=_post_shapes(B, N, D),
        scratch_shapes=[pltpu.VMEM((tm + 2 * CONV_HALO, D), F32), pltpu.VMEM((tm, D), F32)],
        compiler_params=_cparams(("parallel", "parallel")),
        name="conv",
    )(x, u, u, u, wdw, bdw, lng, lnb, w2, b2, g1, g2n, sh2, sc2, rwa, rwb)


def _rope_tables(n):
    rows = n // GRID_W
    row = jnp.repeat(jnp.arange(rows), GRID_W).astype(F32)
    col = jnp.tile(jnp.arange(GRID_W), rows).astype(F32)
    per_axis = QK_ROPE // 2
    inv_freq = 1.0 / (ROPE_THETA ** (jnp.arange(0, per_axis, 2, dtype=F32) / per_axis))
    ang = jnp.stack([row[:, None] * inv_freq, col[:, None] * inv_freq], axis=1)
    cos, sin = jnp.cos(ang), jnp.sin(ang)
    cos32 = jnp.broadcast_to(cos[:, :, None, :], (n, 2, 2, QK_ROPE // 4)).reshape(n, QK_ROPE)
    sin32 = jnp.stack([-sin, sin], axis=2).reshape(n, QK_ROPE)
    return cos32, sin32


def _head_tables(cos32, sin32, lead, scale):
    n = cos32.shape[0]
    cosf = jnp.concatenate([jnp.full((n, QK_NOPE), lead, F32), cos32, jnp.zeros((n, QK_ROPE), F32)], axis=1)
    sinf = jnp.concatenate([jnp.zeros((n, QK_NOPE), F32), sin32, jnp.zeros((n, QK_ROPE), F32)], axis=1)
    return cosf * scale, sinf * scale


def _layer0_weights(mix_w_in, w_uq, w_ukv):
    D = mix_w_in.shape[0]
    perm = jnp.arange(QK_ROPE) ^ (QK_ROPE // 4)
    rope_cols = mix_w_in[:, KV_OFF + KV_LORA:]
    wmix = jnp.concatenate([mix_w_in[:, :KV_OFF + KV_LORA], jnp.zeros((D, QK_NOPE), F32),
                            rope_cols, rope_cols[:, perm]], axis=1).astype(BF16)
    wq3 = w_uq.reshape(Q_LORA, MLA_HEADS, QK_DIM)
    wq = jnp.concatenate([wq3, wq3[:, :, QK_NOPE:][:, :, perm]], axis=2).reshape(Q_LORA, MLA_HEADS * HEAD_PAD)
    wkv3 = w_ukv.reshape(KV_LORA, MLA_HEADS, QK_NOPE + V_DIM)
    zpad = jnp.zeros((KV_LORA, MLA_HEADS, HEAD_PAD - QK_NOPE), F32)
    wk = jnp.concatenate([wkv3[:, :, :QK_NOPE], zpad], axis=2).reshape(KV_LORA, MLA_HEADS * HEAD_PAD)
    wvt = wkv3[:, :, QK_NOPE:].reshape(KV_LORA, MLA_HEADS * V_DIM).T
    return wmix, wq.astype(BF16), wk.astype(BF16), wvt.astype(BF16)


def _router_weights(rw):
    D, E = rw.shape
    rwp = jnp.concatenate([rw, jnp.zeros((D, LANES - E), F32)], axis=1)
    hi = rwp.astype(BF16)
    lo = (rwp - hi.astype(F32)).astype(BF16)
    return jnp.concatenate([hi, lo], axis=1), hi


def kernel(x, c, ctx, c_ctx, ada_w, ada_b, norm1_g, norm2_g, mix_w_in, pool_w, pool_scale, q_norm_g, kv_norm_g, w_uq, w_ukv, mix_w_out, conv_w_pw1, conv_b_pw1, conv_w_dw, conv_b_dw, conv_ln_g, conv_ln_b, conv_w_pw2, conv_b_pw2, router_w, exp_wg, exp_wu, exp_wd, final_g):
    B, N, D = x.shape
    LC = ctx.shape[1]

    cc = jnp.concatenate([c, c_ctx[None, :], jnp.zeros((16 - B - 1, D), F32)], axis=0)
    mods = _ada(cc, ada_w, ada_b)

    def mod(l, k):
        return mods[l, :B, k * D:(k + 1) * D].reshape(B, 1, D)

    def mod_ctx(l, k):
        return jnp.broadcast_to(mods[l, B, k * D:(k + 1) * D].reshape(1, 1, D), (B, 1, D))

    row = lambda v: v.reshape(1, -1)

    wmix, wq, wk, wvt = _layer0_weights(mix_w_in[0], w_uq[0], w_ukv[0])
    qscale = (QK_DIM ** -0.5) * math.log2(math.e)
    cos32, sin32 = _rope_tables(N)
    cq, sq = _head_tables(cos32, sin32, 1.0, qscale)
    ck, sk = _head_tables(cos32, sin32, 0.0, 1.0)
    one32, zero32 = jnp.ones((LC, QK_ROPE), F32), jnp.zeros((LC, QK_ROPE), F32)
    cqc, sqc = _head_tables(one32, zero32, 1.0, qscale)
    ckc, skc = _head_tables(one32, zero32, 0.0, 1.0)
    lw = (row(norm1_g[0]), wmix, row(q_norm_g[0]), wq, row(kv_norm_g[0]), wk, wvt)
    q, k, vt, zp = _mixin(x, mod(0, 0), mod(0, 1), *lw, cq, sq, ck, sk, TOKEN_TILE)
    _, kc, vtc, _ = _mixin(ctx, mod_ctx(0, 0), mod_ctx(0, 1), *lw, cqc, sqc, ckc, skc, LC)
    attn = _attn(q, k, vt, kc, vtc)
    rwa, rwb = _router_weights(router_w[0])
    x, h2, aff = _mixout(x, zp, attn, pool_w[0].astype(BF16), row(pool_scale[0]), mix_w_out[0].astype(BF16),
                         mod(0, 2), row(norm2_g[0]), mod(0, 3), mod(0, 4), rwa, rwb)
    x = _moe(x, h2, aff, mod(0, 5), exp_wg, exp_wu, exp_wd, 0, row(final_g), False)

    u = _pw1(x, mod(1, 0), mod(1, 1), row(norm1_g[1]), conv_w_pw1[0].astype(BF16), row(conv_b_pw1[0]))
    wdw = jnp.concatenate([conv_w_dw[0], jnp.zeros((1, D), F32)], axis=0)
    rwa, rwb = _router_weights(router_w[1])
    x, h2, aff = _conv(x, u, wdw, row(conv_b_dw[0]), row(conv_ln_g[0]), row(conv_ln_b[0]),
                       conv_w_pw2[0].astype(BF16), row(conv_b_pw2[0]),
                       mod(1, 2), row(norm2_g[1]), mod(1, 3), mod(1, 4), rwa, rwb)
    return _moe(x, h2, aff, mod(1, 5), exp_wg, exp_wu, exp_wd, 1, row(final_g), True)
```

```python
import functools
import math

import jax
import jax.numpy as jnp
from jax import lax
from jax.experimental import pallas as pl
from jax.experimental.pallas import tpu as pltpu

F32 = jnp.float32
BF16 = jnp.bfloat16

D_MODEL = 1024
BATCH = 8
SEQ = 4096
CTX_LEN = 256
GRID_W = 64
POOL_WINDOWS = (2, 4, 8, 16)
POOL_GROUP_DIM = 128
POOL_DIM = 512
MLA_HEADS = 8
QK_NOPE = 64
QK_ROPE = 32
V_DIM = 64
Q_LORA = 256
KV_LORA = 128
QK_DIM = QK_NOPE + QK_ROPE
KV_OFF = POOL_DIM + Q_LORA
ROPE_THETA = 10000.0
CONV_WIDTH = 31
N_EXPERTS = 16
CAP = 2 * SEQ // N_EXPERTS
EPS = 1e-6

LANES = 128
HEAD_PAD = 128
POOL_HALO = 8
CONV_HALO = 16
CONV_ROWS = 64
TOKEN_TILE = 512
Q_TILE = 512
ATTN_KEY_CHUNK = 512
MOE_BLOCK = 256
MOE_STEP = 512
SLOT_WIN = 64
SLOT_ALIGN = 16
WIN_OVERFLOW = 4096.0
FFN_BATCH = 2
VMEM_LIMIT = 52 * 1024 * 1024


def _cparams(sem):
    return pltpu.CompilerParams(dimension_semantics=sem, vmem_limit_bytes=VMEM_LIMIT)


def _rms(xf, g):
    ms = jnp.mean(xf * xf, axis=-1, keepdims=True)
    return xf * lax.rsqrt(ms + EPS) * g


def _dot(a, b):
    return jnp.dot(a, b, preferred_element_type=F32)


def _dot_nt(a, b):
    return lax.dot_general(a, b, (((1,), (1,)), ((), ())), preferred_element_type=F32)


def _ada_kernel(c_ref, w_ref, b_ref, o_ref):
    cv = c_ref[...]
    s = cv * jax.nn.sigmoid(cv)
    o_ref[0] = jnp.dot(s, w_ref[0], precision=lax.Precision.HIGHEST,
                       preferred_element_type=F32) + b_ref[0]


def _ada(cc, ada_w, ada_b):
    L, D, D6 = ada_w.shape
    tn = 1536
    return pl.pallas_call(
        _ada_kernel,
        grid=(L, D6 // tn),
        in_specs=[pl.BlockSpec((16, D), lambda l, j: (0, 0)),
                  pl.BlockSpec((1, D, tn), lambda l, j: (l, 0, j)),
                  pl.BlockSpec((1, 1, tn), lambda l, j: (l, 0, j))],
        out_specs=pl.BlockSpec((1, 16, tn), lambda l, j: (l, 0, j)),
        out_shape=jax.ShapeDtypeStruct((L, 16, D6), F32),
        compiler_params=_cparams(("parallel", "parallel")),
        name="ada",
    )(cc, ada_w, ada_b.reshape(L, 1, D6))


def _mixin_kernel(x_ref, sh_ref, sc_ref, g_ref, wmix_ref, qg_ref, wq_ref, kvg_ref, wk_ref, wvt_ref,
                  cq_ref, sq_ref, ck_ref, sk_ref, q_ref, k_ref, vt_ref, zp_ref):
    h = _rms(x_ref[0], g_ref[...]) * (1.0 + sc_ref[0]) + sh_ref[0]
    z = _dot(h.astype(BF16), wmix_ref[...])
    zp_ref[0] = z[:, :POOL_DIM]
    zq = _rms(z[:, POOL_DIM:KV_OFF], qg_ref[...]).astype(BF16)
    qa = _dot(zq, wq_ref[...])
    cq = cq_ref[...]
    sq = sq_ref[...]
    for hh in range(MLA_HEADS):
        qh = qa[:, hh * HEAD_PAD:(hh + 1) * HEAD_PAD]
        q_ref[0, hh] = (qh * cq + pltpu.roll(qh, HEAD_PAD - QK_ROPE, 1) * sq).astype(BF16)
    zkv = _rms(z[:, KV_OFF:KV_OFF + KV_LORA], kvg_ref[...]).astype(BF16)
    ka = _dot(zkv, wk_ref[...])
    zr = z[:, KV_OFF + KV_LORA:]
    kr = zr * ck_ref[...] + pltpu.roll(zr, HEAD_PAD - QK_ROPE, 1) * sk_ref[...]
    for hh in range(MLA_HEADS):
        k_ref[0, hh] = (ka[:, hh * HEAD_PAD:(hh + 1) * HEAD_PAD] + kr).astype(BF16)
    vt_ref[0] = _dot_nt(wvt_ref[...], zkv).astype(BF16)


def _mixin(x, sh, sc, g, wmix, qg, wq, kvg, wk, wvt, cq, sq, ck, sk, tm):
    B, N, D = x.shape
    H = MLA_HEADS
    vec = lambda: pl.BlockSpec((1, 1, D), lambda b, i: (b, 0, 0))
    full = lambda a: pl.BlockSpec(a.shape, lambda b, i: (0,) * a.ndim)
    tab = lambda: pl.BlockSpec((tm, LANES), lambda b, i: (i, 0))
    hd = lambda: pl.BlockSpec((1, H, tm, HEAD_PAD), lambda b, i: (b, 0, i, 0))
    return pl.pallas_call(
        _mixin_kernel,
        grid=(B, N // tm),
        in_specs=[pl.BlockSpec((1, tm, D), lambda b, i: (b, i, 0)), vec(), vec(), full(g), full(wmix),
                  full(qg), full(wq), full(kvg), full(wk), full(wvt), tab(), tab(), tab(), tab()],
        out_specs=[hd(), hd(), pl.BlockSpec((1, H * V_DIM, tm), lambda b, i: (b, 0, i)),
                   pl.BlockSpec((1, tm, POOL_DIM), lambda b, i: (b, i, 0))],
        out_shape=[jax.ShapeDtypeStruct((B, H, N, HEAD_PAD), BF16)] * 2
        + [jax.ShapeDtypeStruct((B, H * V_DIM, N), BF16), jax.ShapeDtypeStruct((B, N, POOL_DIM), F32)],
        compiler_params=_cparams(("parallel", "parallel")),
        name="mixin",
    )(x, sh, sc, g, wmix, qg, wq, kvg, wk, wvt, cq, sq, ck, sk)


def _col_reduce(x, op):
    for group in (256, 64, 8):
        if x.shape[0] > group:
            x = op(x.reshape(x.shape[0] // group, group, x.shape[1]), axis=0)
    return op(x, axis=0, keepdims=True)


def _attn_kernel(q_ref, qn_ref, k_ref, vt_ref, kc_ref, vtc_ref, o_ref, s0_ref, s1_ref, m0_ref):
    tq = q_ref.shape[2]
    lc = kc_ref.shape[2]
    ck = ATTN_KEY_CHUNK
    nchunk = 1 + k_ref.shape[2] // ck

    def key_rows(c):
        return slice(0, lc) if c == 0 else slice(lc + (c - 1) * ck, lc + c * ck)

    def fill(dst_ref, qr, j, c, m):
        keys = kc_ref[0, j] if c == 0 else k_ref[0, j, (c - 1) * ck:c * ck, :]
        s = _dot_nt(keys, qr[0, j])
        dst_ref[key_rows(c), :] = s
        r = _col_reduce(s, jnp.max)
        return r if m is None else jnp.maximum(m, r)

    def consume(src_ref, j, c, m, acc, l):
        rows = slice(j * V_DIM, (j + 1) * V_DIM)
        vals = vtc_ref[0, rows, :] if c == 0 else vt_ref[0, rows, (c - 1) * ck:c * ck]
        p = jnp.exp2(src_ref[key_rows(c), :] - m)
        return acc + _dot(vals, p.astype(BF16)), l + _col_reduce(p, jnp.sum)

    @pl.when(pl.program_id(2) == 0)
    def _():
        m = None
        for c in range(nchunk):
            m = fill(s0_ref, q_ref, 0, c, m)
        m0_ref[...] = m

    zero = (jnp.zeros((V_DIM, tq), F32), jnp.zeros((1, tq), F32))
    m0 = m0_ref[...]
    m1 = None
    acc0, l0 = zero
    for c in range(nchunk):
        m1 = fill(s1_ref, q_ref, 1, c, m1)
        acc0, l0 = consume(s0_ref, 0, c, m0, acc0, l0)
    mn = None
    acc1, l1 = zero
    for c in range(nchunk):
        mn = fill(s0_ref, qn_ref, 0, c, mn)
        acc1, l1 = consume(s1_ref, 1, c, m1, acc1, l1)
    m0_ref[...] = mn
    o_ref[0] = jnp.concatenate([acc0 / l0, acc1 / l1], axis=0).T.astype(BF16)


def _attn(q, k, vt, kc, vtc):
    B, H, N, P = q.shape
    LC = kc.shape[2]
    tq = Q_TILE
    nq = N // tq
    kspec = lambda n: pl.BlockSpec((1, 2, n, P), lambda b, p, i: (b, p, 0, 0))
    vspec = lambda n: pl.BlockSpec((1, 2 * V_DIM, n), lambda b, p, i: (b, p, 0))
    return pl.pallas_call(
        _attn_kernel,
        grid=(B, H // 2, nq),
        in_specs=[pl.BlockSpec((1, 2, tq, P), lambda b, p, i: (b, p, i, 0)),
                  pl.BlockSpec((1, 2, tq, P), lambda b, p, i: (b, p, jnp.minimum(i + 1, nq - 1), 0)),
                  kspec(N), vspec(N), kspec(LC), vspec(LC)],
        out_specs=pl.BlockSpec((1, tq, LANES), lambda b, p, i: (b, i, p)),
        out_shape=jax.ShapeDtypeStruct((B, N, MLA_HEADS * V_DIM), BF16),
        scratch_shapes=[pltpu.VMEM((LC + N, tq), F32), pltpu.VMEM((LC + N, tq), F32), pltpu.VMEM((1, tq), F32)],
        compiler_params=_cparams(("parallel", "parallel", "arbitrary")),
        name="attn",
    )(q, q, k, vt, kc, vtc)


def _post(xn, g2n_ref, sh2_ref, sc2_ref, rwa_ref, rwb_ref, xo_ref, h2_ref, aff_ref):
    xo_ref[0] = xn
    h2 = _rms(xn, g2n_ref[...]) * (1.0 + sc2_ref[0]) + sh2_ref[0]
    hi = h2.astype(BF16)
    lo = (h2 - hi.astype(F32)).astype(BF16)
    h2_ref[0] = hi
    la = _dot(hi, rwa_ref[...])
    lb = _dot(lo, rwb_ref[...])
    logits = la[:, :LANES] + la[:, LANES:] + lb
    lane = lax.broadcasted_iota(jnp.int32, logits.shape, 1)
    logits = jnp.where(lane < N_EXPERTS, logits, -1e30)
    m = jnp.max(logits, axis=-1, keepdims=True)
    e = jnp.exp(logits - m)
    aff_ref[0] = e / jnp.sum(e, axis=-1, keepdims=True)


def _post_specs(D, tm):
    vec = lambda: pl.BlockSpec((1, 1, D), lambda b, i: (b, 0, 0))
    in_specs = [pl.BlockSpec((1, D), lambda b, i: (0, 0)), vec(), vec(),
                pl.BlockSpec((D, 2 * LANES), lambda b, i: (0, 0)),
                pl.BlockSpec((D, LANES), lambda b, i: (0, 0))]
    out_specs = [pl.BlockSpec((1, tm, D), lambda b, i: (b, i, 0)),
                 pl.BlockSpec((1, tm, D), lambda b, i: (b, i, 0)),
                 pl.BlockSpec((1, tm, LANES), lambda b, i: (b, i, 0))]
    return in_specs, out_specs


def _post_shapes(B, N, D):
    return [jax.ShapeDtypeStruct((B, N, D), F32), jax.ShapeDtypeStruct((B, N, D), BF16),
            jax.ShapeDtypeStruct((B, N, LANES), F32)]


def _mixout_kernel(x_ref, zp_ref, zprev_ref, znext_ref, attn_ref, pw_ref, ps_ref, wout_ref, g1_ref,
                   g2n_ref, sh2_ref, sc2_ref, rwa_ref, rwb_ref, xo_ref, h2_ref, aff_ref, ext_ref):
    i = pl.program_id(1)
    nt = pl.num_programs(1)
    tm = x_ref.shape[1]
    hl = POOL_HALO
    ext_ref[0:hl] = jnp.where(i > 0, zprev_ref[0], 0.0)
    ext_ref[hl:hl + tm] = zp_ref[0]
    ext_ref[hl + tm:] = jnp.where(i < nt - 1, znext_ref[0], 0.0)
    t = i * tm + lax.broadcasted_iota(jnp.int32, (tm, 1), 0)
    ys = []
    for g, w in enumerate(POOL_WINDOWS):
        cols = slice(g * POOL_GROUP_DIM, (g + 1) * POOL_GROUP_DIM)
        s = ext_ref[:, cols]
        n = s.shape[0]
        span = 1
        while span < w:
            s = s + pltpu.roll(s, n - span, 0)
            span *= 2
        first = hl - w // 2
        s = (pltpu.roll(s, n - first, 0) if first else s)[:tm]
        lo = jnp.maximum(t - w // 2, 0)
        hi = jnp.minimum(t - w // 2 + w, SEQ)
        d = s / (hi - lo).astype(F32) - ext_ref[hl:hl + tm, cols]
        ys.append(_dot(d.astype(BF16), pw_ref[g]))
    pool = (jnp.concatenate(ys, axis=-1) * ps_ref[...]).astype(BF16)
    y = _dot(pool, wout_ref[0:POOL_DIM]) + _dot(attn_ref[0], wout_ref[POOL_DIM:])
    xn = x_ref[0] + g1_ref[0] * y
    _post(xn, g2n_ref, sh2_ref, sc2_ref, rwa_ref, rwb_ref, xo_ref, h2_ref, aff_ref)


def _mixout(x, zp, attn, pw, ps, wout, g1, g2n, sh2, sc2, rwa, rwb):
    B, N, D = x.shape
    tm = TOKEN_TILE
    hb = tm // POOL_HALO
    nhb = N // POOL_HALO
    vec = lambda: pl.BlockSpec((1, 1, D), lambda b, i: (b, 0, 0))
    pin, pout = _post_specs(D, tm)
    return pl.pallas_call(
        _mixout_kernel,
        grid=(B, N // tm),
        in_specs=[pl.BlockSpec((1, tm, D), lambda b, i: (b, i, 0)),
                  pl.BlockSpec((1, tm, POOL_DIM), lambda b, i: (b, i, 0)),
                  pl.BlockSpec((1, POOL_HALO, POOL_DIM), lambda b, i: (b, jnp.maximum(i * hb - 1, 0), 0)),
                  pl.BlockSpec((1, POOL_HALO, POOL_DIM),
                               lambda b, i: (b, jnp.minimum((i + 1) * hb, nhb - 1), 0)),
                  pl.BlockSpec((1, tm, POOL_DIM), lambda b, i: (b, i, 0)),
                  pl.BlockSpec(pw.shape, lambda b, i: (0, 0, 0)),
                  pl.BlockSpec((1, POOL_DIM), lambda b, i: (0, 0)),
                  pl.BlockSpec(wout.shape, lambda b, i: (0, 0)),
                  vec()] + pin,
        out_specs=pout,
        out_shape=_post_shapes(B, N, D),
        scratch_shapes=[pltpu.VMEM((tm + 2 * POOL_HALO, POOL_DIM), F32)],
        compiler_params=_cparams(("parallel", "parallel")),
        name="mixout",
    )(x, zp, zp, zp, attn, pw, ps, wout, g1, g2n, sh2, sc2, rwa, rwb)


def _topk_kernel(aff_ref, rank_ref, rank_t_ref, gp_ref, wa_ref, wc_ref):
    n = aff_ref.shape[1]
    nchunk = n // LANES

    def search(it, thr):
        bits = pltpu.bitcast(aff_ref[0], jnp.int32)
        cand = thr | (jnp.int32(1) << (30 - it))
        cnt = _col_reduce(jnp.where(bits >= cand, 1.0, 0.0), jnp.sum)
        return jnp.where(cnt >= CAP, cand, thr)

    thr = lax.fori_loop(0, 31, search, jnp.zeros((1, LANES), jnp.int32))

    ri = lax.broadcasted_iota(jnp.int32, (LANES, LANES), 0)
    ci = lax.broadcasted_iota(jnp.int32, (LANES, LANES), 1)
    ltri = jnp.where(ri > ci, 1.0, 0.0).astype(BF16)

    def prefix(mask):
        run = jnp.zeros((1, LANES), F32)
        outs = []
        for c in range(nchunk):
            mc = mask[c * LANES:(c + 1) * LANES]
            outs.append(_dot(ltri, mc.astype(BF16)) + run)
            run = run + _col_reduce(mc, jnp.sum)
        return jnp.concatenate(outs, axis=0), run

    a = aff_ref[0]
    bits = pltpu.bitcast(a, jnp.int32)
    gt = jnp.where(bits > thr, 1.0, 0.0)
    eq = jnp.where(bits == thr, 1.0, 0.0)
    need = CAP - _col_reduce(gt, jnp.sum)
    pe, _ = prefix(eq)
    sel = gt + eq * jnp.where(pe < need, 1.0, 0.0)
    pref, total = prefix(sel)
    rank = jnp.where(sel > 0.0, pref, -1.0)
    rank_ref[0] = rank
    for c in range(nchunk):
        rank_t_ref[0, :, c * LANES:(c + 1) * LANES] = rank[c * LANES:(c + 1) * LANES].T[:N_EXPERTS]
    gate = jnp.where(sel > 0.0, a, 0.0)
    g_hi = gate.astype(BF16)
    r1 = gate - g_hi.astype(F32)
    g_mid = r1.astype(BF16)
    g_lo = (r1 - g_mid.astype(F32)).astype(BF16)
    gp_ref[0] = jnp.concatenate([g_hi, g_mid, g_lo], axis=-1)
    lo =[pref[j * MOE_BLOCK:j * MOE_BLOCK + 1] for j in range(n // MOE_BLOCK)] + [total]
    wa, wc = [], []
    for j in range(n // MOE_BLOCK):
        start = jnp.minimum(jnp.floor(lo[j] * (1.0 / SLOT_ALIGN)) * SLOT_ALIGN, float(CAP - SLOT_WIN))
        wa.append(start)
        wc.append(start + jnp.where(lo[j + 1] > start + SLOT_WIN, WIN_OVERFLOW, 0.0))
    wa_ref[0] = jnp.concatenate(wa, axis=0)
    wc_ref[0] = jnp.concatenate(wc, axis=0)


def _topk(aff):
    B, N, E = aff.shape
    nj = N // MOE_BLOCK
    spec = lambda: pl.BlockSpec((1, N, E), lambda b: (b, 0, 0))
    wspec = lambda: pl.BlockSpec((1, nj, E), lambda b: (b, 0, 0))
    return pl.pallas_call(
        _topk_kernel,
        grid=(B,),
        in_specs=[spec()],
        out_specs=[spec(), pl.BlockSpec((1, N_EXPERTS, N), lambda b: (b, 0, 0)),
                   pl.BlockSpec((1, N, 3 * E), lambda b: (b, 0, 0)), wspec(), wspec()],
        out_shape=[jax.ShapeDtypeStruct((B, N, E), F32), jax.ShapeDtypeStruct((B, N_EXPERTS, N), F32),
                   jax.ShapeDtypeStruct((B, N, 3 * E), BF16),
                   jax.ShapeDtypeStruct((B, nj, E), F32), jax.ShapeDtypeStruct((B, nj, E), F32)],
        compiler_params=_cparams(("parallel",)),
        name="topk",
    )(aff)


def _gather_kernel(wa_s, ovf_s, rank_t_ref, h2_ref, gp_ref, xs_ref, gs_ref):
    b = pl.program_id(0)
    jg = pl.program_id(1)
    tb = MOE_BLOCK
    nsub = h2_ref.shape[1] // tb

    @pl.when(jg == 0)
    def _():
        xs_ref[...] = jnp.zeros_like(xs_ref)
        gs_ref[...] = jnp.zeros_like(gs_ref)

    def gate_sum(g):
        return g[:, :LANES] + g[:, LANES:2 * LANES] + g[:, 2 * LANES:]

    sub = lax.broadcasted_iota(jnp.int32, (SLOT_WIN, tb), 0).astype(F32)

    def routing_block(sj):
        base = ((b * pl.num_programs(1) + jg) * nsub + sj) * N_EXPERTS
        tok = slice(sj * tb, (sj + 1) * tb)
        ps = []
        for e in range(N_EXPERTS):
            wcmp = (wa_s[base + e] + ovf_s[base + e] * int(WIN_OVERFLOW)).astype(F32)
            ps.append(jnp.where(rank_t_ref[0, e:e + 1, tok] - wcmp == sub, 1.0, 0.0).astype(BF16))
        pcat = jnp.concatenate(ps, axis=0)
        res = _dot(pcat, h2_ref[0, tok, :])
        gres = gate_sum(_dot(pcat, gp_ref[0, tok, :]))
        for e in range(N_EXPERTS):
            wa = pl.multiple_of(wa_s[base + e], SLOT_ALIGN)
            rows = slice(e * SLOT_WIN, (e + 1) * SLOT_WIN)
            xs_ref[0, e, pl.ds(wa, SLOT_WIN), :] += res[rows].astype(BF16)
            gs_ref[0, e, pl.ds(wa, SLOT_WIN), :] += gres[rows]

        def overflow(e, carry):
            @pl.when(ovf_s[base + e] != 0)
            def _():
                r = rank_t_ref[0, pl.ds(e, 1), tok]
                for k in range(CAP // SLOT_WIN):
                    p = jnp.where(r - float(k * SLOT_WIN) == sub, 1.0, 0.0).astype(BF16)
                    rows = slice(k * SLOT_WIN, (k + 1) * SLOT_WIN)
                    xs_ref[0, e, rows, :] += _dot(p, h2_ref[0, tok, :]).astype(BF16)
                    gs_ref[0, e, rows, :] += gate_sum(_dot(p, gp_ref[0, tok, :]))
            return carry

        lax.fori_loop(0, N_EXPERTS, overflow, 0)

    for sj in range(nsub):
        routing_block(sj)


def _gather(wa_s, ovf_s, rank_t, h2, gp):
    B, N, D = h2.shape
    E = N_EXPERTS
    tb = MOE_STEP
    return pl.pallas_call(
        _gather_kernel,
        grid_spec=pltpu.PrefetchScalarGridSpec(
            num_scalar_prefetch=2,
            grid=(B, N // tb),
            in_specs=[pl.BlockSpec((1, E, tb), lambda b, j, *_: (b, 0, j)),
                      pl.BlockSpec((1, tb, D), lambda b, j, *_: (b, j, 0)),
                      pl.BlockSpec((1, tb, 3 * LANES), lambda b, j, *_: (b, j, 0))],
            out_specs=[pl.BlockSpec((1, E, CAP, D), lambda b, j, *_: (b, 0, 0, 0)),
                       pl.BlockSpec((1, E, CAP, LANES), lambda b, j, *_: (b, 0, 0, 0))]),
        out_shape=[jax.ShapeDtypeStruct((B, E, CAP, D), BF16), jax.ShapeDtypeStruct((B, E, CAP, LANES), F32)],
        compiler_params=_cparams(("parallel", "arbitrary")),
        name="gather",
    )(wa_s, ovf_s, rank_t, h2, gp)


def _ffn_kernel(xs_ref, gs_ref, wg_ref, wu_ref, wd_ref, y_ref, wgb_ref, wub_ref, wdb_ref):
    e = pl.program_id(0)

    @pl.when(pl.program_id(1) == 0)
    def _():
        wgb_ref[...] = wg_ref[0, 0].astype(BF16)
        wub_ref[...] = wu_ref[0, 0].astype(BF16)
        wdb_ref[...] = wd_ref[0, 0].astype(BF16)

    lane = lax.broadcasted_iota(jnp.int32, gs_ref.shape[2:], 1)
    for s in range(xs_ref.shape[0]):
        xs = xs_ref[s, 0]
        a = _dot(xs, wgb_ref[...])
        u = _dot(xs, wub_ref[...])
        hm = (a * jax.nn.sigmoid(a) * u).astype(BF16)
        gate = jnp.sum(jnp.where(lane == e, gs_ref[s, 0], 0.0), axis=-1, keepdims=True)
        y_ref[s, 0] = (_dot(hm, wdb_ref[...]) * gate).astype(BF16)


def _ffn(xs, gs, wg, wu, wd, layer):
    B, E, C, D = xs.shape
    F = wg.shape[3]
    nb = FFN_BATCH
    return pl.pallas_call(
        _ffn_kernel,
        grid=(E, B // nb),
        in_specs=[pl.BlockSpec((nb, 1, C, D), lambda e, b: (b, e, 0, 0)),
                  pl.BlockSpec((nb, 1, C, LANES), lambda e, b: (b, e, 0, 0)),
                  pl.BlockSpec((1, 1, D, F), lambda e, b: (layer, e, 0, 0)),
                  pl.BlockSpec((1, 1, D, F), lambda e, b: (layer, e, 0, 0)),
                  pl.BlockSpec((1, 1, F, D), lambda e, b: (layer, e, 0, 0))],
        out_specs=pl.BlockSpec((nb, 1, C, D), lambda e, b: (b, e, 0, 0)),
        out_shape=jax.ShapeDtypeStruct((B, E, C, D), BF16),
        scratch_shapes=[pltpu.VMEM((D, F), BF16), pltpu.VMEM((D, F), BF16), pltpu.VMEM((F, D), BF16)],
        compiler_params=_cparams(("arbitrary", "arbitrary")),
        name="ffn",
    )(xs, gs, wg, wu, wd)


def _combine_kernel(wa_s, ovf_s, y_ref, rank_ref, wc_ref, spread_ref, x_ref, g2_ref, fg_ref, o_ref,
                    ycat_ref, acc_ref, *, final):
    b = pl.program_id(0)
    jg = pl.program_id(1)
    tb = MOE_BLOCK
    nsub = x_ref.shape[1] // tb
    kc = N_EXPERTS * SLOT_WIN

    def routing_block(sj):
        j = jg * nsub + sj
        base = (b * (pl.num_programs(1) * nsub) + j) * N_EXPERTS
        tok = slice(sj * tb, (sj + 1) * tb)
        for e in range(N_EXPERTS):
            wa = pl.multiple_of(wa_s[base + e], SLOT_ALIGN)
            ycat_ref[e * SLOT_WIN:(e + 1) * SLOT_WIN, :] = y_ref[0, e, pl.ds(wa, SLOT_WIN), :]
        rel = rank_ref[0, tok, :] - wc_ref[0, pl.ds(j, 1), :]
        rel = jnp.where(rel >= 0.0, jnp.where(rel < float(SLOT_WIN), rel, float(SLOT_WIN)), float(SLOT_WIN))
        spread = _dot(rel.astype(BF16), spread_ref[...])
        col = (lax.broadcasted_iota(jnp.int32, (tb, kc), 1) % SLOT_WIN).astype(F32)
        pt = jnp.where(spread == col, 1.0, 0.0).astype(BF16)
        acc_ref[...] = _dot(pt, ycat_ref[...])

        def overflow(e, carry):
            @pl.when(ovf_s[base + e] != 0)
            def _():
                lane = lax.broadcasted_iota(jnp.int32, (tb, LANES), 1)
                rk = jnp.sum(jnp.where(lane == e, rank_ref[0, tok, :], 0.0), axis=-1, keepdims=True)
                slot = lax.broadcasted_iota(jnp.int32, (tb, SLOT_WIN), 1).astype(F32)
                for k in range(CAP // SLOT_WIN):
                    p = jnp.where(rk - float(k * SLOT_WIN) == slot, 1.0, 0.0).astype(BF16)
                    acc_ref[...] += _dot(p, y_ref[0, e, k * SLOT_WIN:(k + 1) * SLOT_WIN, :])
            return carry

        lax.fori_loop(0, N_EXPERTS, overflow, 0)

        xn = x_ref[0, tok, :] + g2_ref[0] * acc_ref[...]
        if final:
            xn = _rms(xn, fg_ref[...])
        o_ref[0, tok, :] = xn

    for sj in range(nsub):
        routing_block(sj)


def _combine(wa_s, ovf_s, y, rank, wc, x, g2, fg, final):
    B, N, D = x.shape
    E = N_EXPERTS
    tb = MOE_STEP
    kc = E * SLOT_WIN
    spread = (jnp.arange(LANES)[:, None] == jnp.arange(kc)[None, :] // SLOT_WIN).astype(BF16)
    return pl.pallas_call(
        functools.partial(_combine_kernel, final=final),
        grid_spec=pltpu.PrefetchScalarGridSpec(
            num_scalar_prefetch=2,
            grid=(B, N // tb),
            in_specs=[pl.BlockSpec((1, E, CAP, D), lambda b, j, *_: (b, 0, 0, 0)),
                      pl.BlockSpec((1, tb, LANES), lambda b, j, *_: (b, j, 0)),
                      pl.BlockSpec((1, N // MOE_BLOCK, LANES), lambda b, j, *_: (b, 0, 0)),
                      pl.BlockSpec((LANES, kc), lambda b, j, *_: (0, 0)),
                      pl.BlockSpec((1, tb, D), lambda b, j, *_: (b, j, 0)),
                      pl.BlockSpec((1, 1, D), lambda b, j, *_: (b, 0, 0)),
                      pl.BlockSpec((1, D), lambda b, j, *_: (0, 0))],
            out_specs=pl.BlockSpec((1, tb, D), lambda b, j, *_: (b, j, 0)),
            scratch_shapes=[pltpu.VMEM((kc, D), BF16), pltpu.VMEM((MOE_BLOCK, D), F32)]),
        out_shape=jax.ShapeDtypeStruct((B, N, D), F32),
        compiler_params=_cparams(("parallel", "arbitrary")),
        name="combine",
    )(wa_s, ovf_s, y, rank, wc, spread, x, g2, fg)


def _moe(x, h2, aff, g2, wg, wu, wd, layer, fg, final):
    rank, rank_t, gp, wa, wc = _topk(aff)
    wa_s = wa[:, :, :N_EXPERTS].astype(jnp.int32).reshape(-1)
    ovf_s = (wc[:, :, :N_EXPERTS] != wa[:, :, :N_EXPERTS]).astype(jnp.int32).reshape(-1)
    xs, gs = _gather(wa_s, ovf_s, rank_t, h2, gp)
    y = _ffn(xs, gs, wg, wu, wd, layer)
    return _combine(wa_s, ovf_s, y, rank, wc, x, g2, fg, final)


def _pw1_kernel(x_ref, sh_ref, sc_ref, g_ref, w_ref, b_ref, u_ref):
    d = x_ref.shape[2]
    h = _rms(x_ref[0], g_ref[...]) * (1.0 + sc_ref[0]) + sh_ref[0]
    z = _dot(h.astype(BF16), w_ref[...]) + b_ref[...]
    u_ref[0] = z[:, :d] * jax.nn.sigmoid(z[:, d:])


def _pw1(x, sh, sc, g, w, b):
    B, N, D = x.shape
    tm = TOKEN_TILE
    vec = lambda: pl.BlockSpec((1, 1, D), lambda b, i: (b, 0, 0))
    return pl.pallas_call(
        _pw1_kernel,
        grid=(B, N // tm),
        in_specs=[pl.BlockSpec((1, tm, D), lambda b, i: (b, i, 0)), vec(), vec(),
                  pl.BlockSpec((1, D), lambda b, i: (0, 0)),
                  pl.BlockSpec((D, 2 * D), lambda b, i: (0, 0)),
                  pl.BlockSpec((1, 2 * D), lambda b, i: (0, 0))],
        out_specs=pl.BlockSpec((1, tm, D), lambda b, i: (b, i, 0)),
        out_shape=jax.ShapeDtypeStruct((B, N, D), F32),
        compiler_params=_cparams(("parallel", "parallel")),
        name="pw1",
    )(x, sh, sc, g, w, b)


def _conv_kernel(x_ref, u_ref, uprev_ref, unext_ref, wdw_ref, bdw_ref, lng_ref, lnb_ref, w2_ref, b2_ref,
                 g1_ref, g2n_ref, sh2_ref, sc2_ref, rwa_ref, rwb_ref, xo_ref, h2_ref, aff_ref, ext_ref, cv_ref):
    i = pl.program_id(1)
    nt = pl.num_programs(1)
    tm = x_ref.shape[1]
    hl = CONV_HALO
    ext_ref[0:hl] = jnp.where(i > 0, uprev_ref[0], 0.0)
    ext_ref[hl:hl + tm] = u_ref[0]
    ext_ref[hl + tm:] = jnp.where(i < nt - 1, unext_ref[0], 0.0)
    lead = hl - CONV_WIDTH // 2

    def conv_rows(rb, carry):
        r0 = pl.multiple_of(rb * CONV_ROWS, CONV_ROWS)
        for lt in range(x_ref.shape[2] // LANES):
            cols = slice(lt * LANES, (lt + 1) * LANES)
            blk = ext_ref[pl.ds(r0, CONV_ROWS + 2 * hl), cols]
            acc = jnp.zeros((CONV_ROWS, LANES), F32)
            for r in range(8):
                view = pltpu.roll(blk, blk.shape[0] - (lead + r), 0)
                for a in range((CONV_WIDTH - r + 7) // 8):
                    k = 8 * a + r
                    acc = acc + view[8 * a:8 * a + CONV_ROWS] * wdw_ref[k:k + 1, cols]
            cv_ref[pl.ds(r0, CONV_ROWS), cols] = acc
        return carry

    lax.fori_loop(0, tm // CONV_ROWS, conv_rows, 0)
    acc = cv_ref[...] + bdw_ref[...]
    mu = jnp.mean(acc, axis=-1, keepdims=True)
    cen = acc - mu
    var = jnp.mean(cen * cen, axis=-1, keepdims=True)
    yn = cen * lax.rsqrt(var + EPS) * lng_ref[...] + lnb_ref[...]
    yn = yn * jax.nn.sigmoid(yn)
    y = _dot(yn.astype(BF16), w2_ref[...]) + b2_ref[...]
    xn = x_ref[0] + g1_ref[0] * y
    _post(xn, g2n_ref, sh2_ref, sc2_ref, rwa_ref, rwb_ref, xo_ref, h2_ref, aff_ref)


def _conv(x, u, wdw, bdw, lng, lnb, w2, b2, g1, g2n, sh2, sc2, rwa, rwb):
    B, N, D = x.shape
    tm = TOKEN_TILE
    hb = tm // CONV_HALO
    nhb = N // CONV_HALO
    vec = lambda: pl.BlockSpec((1, 1, D), lambda b, i: (b, 0, 0))
    row = lambda: pl.BlockSpec((1, D), lambda b, i: (0, 0))
    pin, pout = _post_specs(D, tm)
    return pl.pallas_call(
        _conv_kernel,
        grid=(B, N // tm),
        in_specs=[pl.BlockSpec((1, tm, D), lambda b, i: (b, i, 0)),
                  pl.BlockSpec((1, tm, D), lambda b, i: (b, i, 0)),
                  pl.BlockSpec((1, CONV_HALO, D), lambda b, i: (b, jnp.maximum(i * hb - 1, 0), 0)),
                  pl.BlockSpec((1, CONV_HALO, D), lambda b, i: (b, jnp.minimum((i + 1) * hb, nhb - 1), 0)),
                  pl.BlockSpec(wdw.shape, lambda b, i: (0, 0)),
                  row(), row(), row(),
                  pl.BlockSpec((D, D), lambda b, i: (0, 0)),
                  row(), vec()] + pin,
        out_specs=pout,
        out_shape=_post_shapes(B, N, D),
        scratch_shapes=[pltpu.VMEM((tm + 2 * CONV_HALO, D), F32), pltpu.VMEM((tm, D), F32)],
        compiler_params=_cparams(("parallel", "parallel")),
        name="conv",
    )(x, u, u, u, wdw, bdw, lng, lnb, w2, b2, g1, g2n, sh2, sc2, rwa, rwb)


def _rope_tables(n):
    rows = n // GRID_W
    row = jnp.repeat(jnp.arange(rows), GRID_W).astype(F32)
    col = jnp.tile(jnp.arange(GRID_W), rows).astype(F32)
    per_axis = QK_ROPE // 2
    inv_freq = 1.0 / (ROPE_THETA ** (jnp.arange(0, per_axis, 2, dtype=F32) / per_axis))
    ang = jnp.stack([row[:, None] * inv_freq, col[:, None] * inv_freq], axis=1)
    cos, sin = jnp.cos(ang), jnp.sin(ang)
    cos32 = jnp.broadcast_to(cos[:, :, None, :], (n, 2, 2, QK_ROPE // 4)).reshape(n, QK_ROPE)
    sin32 = jnp.stack([-sin, sin], axis=2).reshape(n, QK_ROPE)
    return cos32, sin32


def _head_tables(cos32, sin32, lead, scale):
    n = cos32.shape[0]
    cosf = jnp.concatenate([jnp.full((n, QK_NOPE), lead, F32), cos32, jnp.zeros((n, QK_ROPE), F32)], axis=1)
    sinf = jnp.concatenate([jnp.zeros((n, QK_NOPE), F32), sin32, jnp.zeros((n, QK_ROPE), F32)], axis=1)
    return cosf * scale, sinf * scale


def _layer0_weights(mix_w_in, w_uq, w_ukv):
    D = mix_w_in.shape[0]
    perm = jnp.arange(QK_ROPE) ^ (QK_ROPE // 4)
    rope_cols = mix_w_in[:, KV_OFF + KV_LORA:]
    wmix = jnp.concatenate([mix_w_in[:, :KV_OFF + KV_LORA], jnp.zeros((D, QK_NOPE), F32),
                            rope_cols, rope_cols[:, perm]], axis=1).astype(BF16)
    wq3 = w_uq.reshape(Q_LORA, MLA_HEADS, QK_DIM)
    wq = jnp.concatenate([wq3, wq3[:, :, QK_NOPE:][:, :, perm]], axis=2).reshape(Q_LORA, MLA_HEADS * HEAD_PAD)
    wkv3 = w_ukv.reshape(KV_LORA, MLA_HEADS, QK_NOPE + V_DIM)
    zpad = jnp.zeros((KV_LORA, MLA_HEADS, HEAD_PAD - QK_NOPE), F32)
    wk = jnp.concatenate([wkv3[:, :, :QK_NOPE], zpad], axis=2).reshape(KV_LORA, MLA_HEADS * HEAD_PAD)
    wvt = wkv3[:, :, QK_NOPE:].reshape(KV_LORA, MLA_HEADS * V_DIM).T
    return wmix, wq.astype(BF16), wk.astype(BF16), wvt.astype(BF16)


def _router_weights(rw):
    D, E = rw.shape
    rwp = jnp.concatenate([rw, jnp.zeros((D, LANES - E), F32)], axis=1)
    hi = rwp.astype(BF16)
    lo = (rwp - hi.astype(F32)).astype(BF16)
    return jnp.concatenate([hi, lo], axis=1), hi


def kernel(x, c, ctx, c_ctx, ada_w, ada_b, norm1_g, norm2_g, mix_w_in, pool_w, pool_scale, q_norm_g, kv_norm_g, w_uq, w_ukv, mix_w_out, conv_w_pw1, conv_b_pw1, conv_w_dw, conv_b_dw, conv_ln_g, conv_ln_b, conv_w_pw2, conv_b_pw2, router_w, exp_wg, exp_wu, exp_wd, final_g):
    B, N, D = x.shape
    LC = ctx.shape[1]

    cc = jnp.concatenate([c, c_ctx[None, :], jnp.zeros((16 - B - 1, D), F32)], axis=0)
    mods = _ada(cc, ada_w, ada_b)

    def mod(l, k):
        return mods[l, :B, k * D:(k + 1) * D].reshape(B, 1, D)

    def mod_ctx(l, k):
        return jnp.broadcast_to(mods[l, B, k * D:(k + 1) * D].reshape(1, 1, D), (B, 1, D))

    row = lambda v: v.reshape(1, -1)

    wmix, wq, wk, wvt = _layer0_weights(mix_w_in[0], w_uq[0], w_ukv[0])
    qscale = (QK_DIM ** -0.5) * math.log2(math.e)
    cos32, sin32 = _rope_tables(N)
    cq, sq = _head_tables(cos32, sin32, 1.0, qscale)
    ck, sk = _head_tables(cos32, sin32, 0.0, 1.0)
    one32, zero32 = jnp.ones((LC, QK_ROPE), F32), jnp.zeros((LC, QK_ROPE), F32)
    cqc, sqc = _head_tables(one32, zero32, 1.0, qscale)
    ckc, skc = _head_tables(one32, zero32, 0.0, 1.0)
    lw = (row(norm1_g[0]), wmix, row(q_norm_g[0]), wq, row(kv_norm_g[0]), wk, wvt)
    q, k, vt, zp = _mixin(x, mod(0, 0), mod(0, 1), *lw, cq, sq, ck, sk, TOKEN_TILE)
    _, kc, vtc, _ = _mixin(ctx, mod_ctx(0, 0), mod_ctx(0, 1), *lw, cqc, sqc, ckc, skc, LC)
    attn = _attn(q, k, vt, kc, vtc)
    rwa, rwb = _router_weights(router_w[0])
    x, h2, aff = _mixout(x, zp, attn, pool_w[0].astype(BF16), row(pool_scale[0]), mix_w_out[0].astype(BF16),
                         mod(0, 2), row(norm2_g[0]), mod(0, 3), mod(0, 4), rwa, rwb)
    x = _moe(x, h2, aff, mod(0, 5), exp_wg, exp_wu, exp_wd, 0, row(final_g), False)

    u = _pw1(x, mod(1, 0), mod(1, 1), row(norm1_g[1]), conv_w_pw1[0].astype(BF16), row(conv_b_pw1[0]))
    wdw = jnp.concatenate([conv_w_dw[0], jnp.zeros((1, D), F32)], axis=0)
    rwa, rwb = _router_weights(router_w[1])
    x, h2, aff = _conv(x, u, wdw, row(conv_b_dw[0]), row(conv_ln_g[0]), row(conv_ln_b[0]),
                       conv_w_pw2[0].astype(BF16), row(conv_b_pw2[0]),
                       mod(1, 2), row(norm2_g[1]), mod(1, 3), mod(1, 4), rwa, rwb)
    return _moe(x, h2, aff, mod(1, 5), exp_wg, exp_wu, exp_wd, 1, row(final_g), True)
```

```python
import functools
import math

import jax
import jax.numpy as jnp
from jax import lax
from jax.experimental import pallas as pl
from jax.experimental.pallas import tpu as pltpu

F32 = jnp.float32
BF16 = jnp.bfloat16

D_MODEL = 1024
BATCH = 8
SEQ = 4096
CTX_LEN = 256
GRID_W = 64
POOL_WINDOWS = (2, 4, 8, 16)
POOL_GROUP_DIM = 128
POOL_DIM = 512
MLA_HEADS = 8
QK_NOPE = 64
QK_ROPE = 32
V_DIM = 64
Q_LORA = 256
KV_LORA = 128
QK_DIM = QK_NOPE + QK_ROPE
KV_OFF = POOL_DIM + Q_LORA
ROPE_THETA = 10000.0
CONV_WIDTH = 31
N_EXPERTS = 16
CAP = 2 * SEQ // N_EXPERTS
EPS = 1e-6

LANES = 128
HEAD_PAD = 128
POOL_HALO = 8
CONV_HALO = 16
CONV_ROWS = 64
TOKEN_TILE = 512
WIDE_TOKEN_TILE = 1024
ATTN_HEADS_PER_STEP = 4
Q_TILE = 512
ATTN_KEY_CHUNK = 512
MOE_BLOCK = 256
MOE_STEP = 512
SLOT_WIN = 64
SLOT_ALIGN = 16
WIN_OVERFLOW = 4096.0
SEARCH_FLOOR = 2.0 ** -62
SEARCH_RESCALE = 2.0 ** 64
FFN_BATCH = 4
VMEM_LIMIT = 56 * 1024 * 1024


def _cparams(sem):
    return pltpu.CompilerParams(dimension_semantics=sem, vmem_limit_bytes=VMEM_LIMIT)


def _rms(xf, g):
    ms = jnp.mean(xf * xf, axis=-1, keepdims=True)
    return xf * lax.rsqrt(ms + EPS) * g


def _dot(a, b):
    return jnp.dot(a, b, preferred_element_type=F32)


def _dot_nt(a, b):
    return lax.dot_general(a, b, (((1,), (1,)), ((), ())), preferred_element_type=F32)


def _ada_kernel(c_ref, w_ref, b_ref, o_ref):
    cv = c_ref[...]
    s = cv * jax.nn.sigmoid(cv)
    o_ref[0] = jnp.dot(s, w_ref[0], precision=lax.Precision.HIGHEST,
                       preferred_element_type=F32) + b_ref[0]


def _ada(cc, ada_w, ada_b):
    L, D, D6 = ada_w.shape
    tn = 1536
    return pl.pallas_call(
        _ada_kernel,
        grid=(L, D6 // tn),
        in_specs=[pl.BlockSpec((16, D), lambda l, j: (0, 0)),
                  pl.BlockSpec((1, D, tn), lambda l, j: (l, 0, j)),
                  pl.BlockSpec((1, 1, tn), lambda l, j: (l, 0, j))],
        out_specs=pl.BlockSpec((1, 16, tn), lambda l, j: (l, 0, j)),
        out_shape=jax.ShapeDtypeStruct((L, 16, D6), F32),
        compiler_params=_cparams(("parallel", "parallel")),
        name="ada",
    )(cc, ada_w, ada_b.reshape(L, 1, D6))


def _mixin_kernel(x_ref, sh_ref, sc_ref, g_ref, wmix_ref, qg_ref, wq_ref, kvg_ref, wk_ref, wvt_ref,
                  cq_ref, sq_ref, ck_ref, sk_ref, q_ref, k_ref, vt_ref, zp_ref):
    h = _rms(x_ref[0], g_ref[...]) * (1.0 + sc_ref[0]) + sh_ref[0]
    z = _dot(h.astype(BF16), wmix_ref[...])
    zp_ref[0] = z[:, :POOL_DIM]
    zq = _rms(z[:, POOL_DIM:KV_OFF], qg_ref[...]).astype(BF16)
    qa = _dot(zq, wq_ref[...])
    cq = cq_ref[...]
    sq = sq_ref[...]
    for hh in range(MLA_HEADS):
        qh = qa[:, hh * HEAD_PAD:(hh + 1) * HEAD_PAD]
        q_ref[0, hh] = (qh * cq + pltpu.roll(qh, HEAD_PAD - QK_ROPE, 1) * sq).astype(BF16)
    zkv = _rms(z[:, KV_OFF:KV_OFF + KV_LORA], kvg_ref[...]).astype(BF16)
    ka = _dot(zkv, wk_ref[...])
    zr = z[:, KV_OFF + KV_LORA:]
    kr = zr * ck_ref[...] + pltpu.roll(zr, HEAD_PAD - QK_ROPE, 1) * sk_ref[...]
    for hh in range(MLA_HEADS):
        k_ref[0, hh] = (ka[:, hh * HEAD_PAD:(hh + 1) * HEAD_PAD] + kr).astype(BF16)
    vt_ref[0] = _dot_nt(wvt_ref[...], zkv).astype(BF16)


def _mixin(x, sh, sc, g, wmix, qg, wq, kvg, wk, wvt, cq, sq, ck, sk, tm):
    B, N, D = x.shape
    H = MLA_HEADS
    vec = lambda: pl.BlockSpec((1, 1, D), lambda b, i: (b, 0, 0))
    full = lambda a: pl.BlockSpec(a.shape, lambda b, i: (0,) * a.ndim)
    tab = lambda: pl.BlockSpec((tm, LANES), lambda b, i: (i, 0))
    hd = lambda: pl.BlockSpec((1, H, tm, HEAD_PAD), lambda b, i: (b, 0, i, 0))
    return pl.pallas_call(
        _mixin_kernel,
        grid=(B, N // tm),
        in_specs=[pl.BlockSpec((1, tm, D), lambda b, i: (b, i, 0)), vec(), vec(), full(g), full(wmix),
                  full(qg), full(wq), full(kvg), full(wk), full(wvt), tab(), tab(), tab(), tab()],
        out_specs=[hd(), hd(), pl.BlockSpec((1, H * V_DIM, tm), lambda b, i: (b, 0, i)),
                   pl.BlockSpec((1, tm, POOL_DIM), lambda b, i: (b, i, 0))],
        out_shape=[jax.ShapeDtypeStruct((B, H, N, HEAD_PAD), BF16)] * 2
        + [jax.ShapeDtypeStruct((B, H * V_DIM, N), BF16), jax.ShapeDtypeStruct((B, N, POOL_DIM), F32)],
        compiler_params=_cparams(("parallel", "parallel")),
        name="mixin",
    )(x, sh, sc, g, wmix, qg, wq, kvg, wk, wvt, cq, sq, ck, sk)


def _col_reduce(x, op):
    for group in (256, 64, 8):
        if x.shape[0] > group:
            x = op(x.reshape(x.shape[0] // group, group, x.shape[1]), axis=0)
    return op(x, axis=0, keepdims=True)


def _attn_kernel(q_ref, qn_ref, k_ref, vt_ref, kc_ref, vtc_ref, o_ref, s0_ref, s1_ref, m0_ref):
    tq = q_ref.shape[2]
    lc = kc_ref.shape[2]
    ck = ATTN_KEY_CHUNK
    nchunk = 1 + k_ref.shape[2] // ck

    def key_rows(c):
        return slice(0, lc) if c == 0 else slice(lc + (c - 1) * ck, lc + c * ck)

    def fill(dst_ref, qr, j, c, m):
        keys = kc_ref[0, j] if c == 0 else k_ref[0, j, (c - 1) * ck:c * ck, :]
        s = _dot_nt(keys, qr[0, j])
        dst_ref[key_rows(c), :] = s
        r = _col_reduce(s, jnp.max)
        return r if m is None else jnp.maximum(m, r)

    def consume(src_ref, j, c, m, acc, l):
        rows = slice(j * V_DIM, (j + 1) * V_DIM)
        vals = vtc_ref[0, rows, :] if c == 0 else vt_ref[0, rows, (c - 1) * ck:c * ck]
        p = jnp.exp2(src_ref[key_rows(c), :] - m)
        return acc + _dot(vals, p.astype(BF16)), l + _col_reduce(p, jnp.sum)

    nh = q_ref.shape[1]
    bufs = (s0_ref, s1_ref)

    @pl.when(pl.program_id(2) == 0)
    def _():
        m = None
        for c in range(nchunk):
            m = fill(s0_ref, q_ref, 0, c, m)
        m0_ref[...] = m

    m_cur = m0_ref[...]
    outs = []
    for h in range(nh):
        nxt_q, nxt_h = (q_ref, h + 1) if h + 1 < nh else (qn_ref, 0)
        m_nxt = None
        acc, l = jnp.zeros((V_DIM, tq), F32), jnp.zeros((1, tq), F32)
        for c in range(nchunk):
            m_nxt = fill(bufs[(h + 1) % 2], nxt_q, nxt_h, c, m_nxt)
            acc, l = consume(bufs[h % 2], h, c, m_cur, acc, l)
        outs.append(acc / l)
        m_cur = m_nxt
    m0_ref[...] = m_cur
    o_ref[0] = jnp.concatenate(outs, axis=0).T.astype(BF16)


def _attn(q, k, vt, kc, vtc):
    B, H, N, P = q.shape
    LC = kc.shape[2]
    tq = Q_TILE
    nq = N // tq
    nh = ATTN_HEADS_PER_STEP
    kspec = lambda n: pl.BlockSpec((1, nh, n, P), lambda b, p, i: (b, p, 0, 0))
    vspec = lambda n: pl.BlockSpec((1, nh * V_DIM, n), lambda b, p, i: (b, p, 0))
    return pl.pallas_call(
        _attn_kernel,
        grid=(B, H // nh, nq),
        in_specs=[pl.BlockSpec((1, nh, tq, P), lambda b, p, i: (b, p, i, 0)),
                  pl.BlockSpec((1, nh, tq, P), lambda b, p, i: (b, p, jnp.minimum(i + 1, nq - 1), 0)),
                  kspec(N), vspec(N), kspec(LC), vspec(LC)],
        out_specs=pl.BlockSpec((1, tq, nh * V_DIM), lambda b, p, i: (b, i, p)),
        out_shape=jax.ShapeDtypeStruct((B, N, MLA_HEADS * V_DIM), BF16),
        scratch_shapes=[pltpu.VMEM((LC + N, tq), F32), pltpu.VMEM((LC + N, tq), F32), pltpu.VMEM((1, tq), F32)],
        compiler_params=_cparams(("parallel", "parallel", "arbitrary")),
        name="attn",
    )(q, q, k, vt, kc, vtc)


def _post(xn, g2n_ref, sh2_ref, sc2_ref, rwa_ref, rwb_ref, xo_ref, h2_ref, aff_ref):
    xo_ref[0] = xn
    h2 = _rms(xn, g2n_ref[...]) * (1.0 + sc2_ref[0]) + sh2_ref[0]
    hi = h2.astype(BF16)
    lo = (h2 - hi.astype(F32)).astype(BF16)
    h2_ref[0] = hi
    la = _dot(hi, rwa_ref[...])
    lb = _dot(lo, rwb_ref[...])
    logits = la[:, :LANES] + la[:, LANES:] + lb
    lane = lax.broadcasted_iota(jnp.int32, logits.shape, 1)
    logits = jnp.where(lane < N_EXPERTS, logits, -1e30)
    m = jnp.max(logits, axis=-1, keepdims=True)
    e = jnp.exp(logits - m)
    aff_ref[0] = e / jnp.sum(e, axis=-1, keepdims=True)


def _post_specs(D, tm):
    vec = lambda: pl.BlockSpec((1, 1, D), lambda b, i: (b, 0, 0))
    in_specs = [pl.BlockSpec((1, D), lambda b, i: (0, 0)), vec(), vec(),
                pl.BlockSpec((D, 2 * LANES), lambda b, i: (0, 0)),
                pl.BlockSpec((D, LANES), lambda b, i: (0, 0))]
    out_specs = [pl.BlockSpec((1, tm, D), lambda b, i: (b, i, 0)),
                 pl.BlockSpec((1, tm, D), lambda b, i: (b, i, 0)),
                 pl.BlockSpec((1, tm, LANES), lambda b, i: (b, i, 0))]
    return in_specs, out_specs


def _post_shapes(B, N, D):
    return [jax.ShapeDtypeStruct((B, N, D), F32), jax.ShapeDtypeStruct((B, N, D), BF16),
            jax.ShapeDtypeStruct((B, N, LANES), F32)]


def _mixout_kernel(x_ref, zp_ref, zprev_ref, znext_ref, attn_ref, pw_ref, ps_ref, wout_ref, g1_ref,
                   g2n_ref, sh2_ref, sc2_ref, rwa_ref, rwb_ref, xo_ref, h2_ref, aff_ref, ext_ref):
    i = pl.program_id(1)
    nt = pl.num_programs(1)
    tm = x_ref.shape[1]
    hl = POOL_HALO
    ext_ref[0:hl] = jnp.where(i > 0, zprev_ref[0], 0.0)
    ext_ref[hl:hl + tm] = zp_ref[0]
    ext_ref[hl + tm:] = jnp.where(i < nt - 1, znext_ref[0], 0.0)
    t = i * tm + lax.broadcasted_iota(jnp.int32, (tm, 1), 0)
    ys = []
    for g, w in enumerate(POOL_WINDOWS):
        cols = slice(g * POOL_GROUP_DIM, (g + 1) * POOL_GROUP_DIM)
        s = ext_ref[:, cols]
        n = s.shape[0]
        span = 1
        while span < w:
            s = s + pltpu.roll(s, n - span, 0)
            span *= 2
        first = hl - w // 2
        s = (pltpu.roll(s, n - first, 0) if first else s)[:tm]
        lo = jnp.maximum(t - w // 2, 0)
        hi = jnp.minimum(t - w // 2 + w, SEQ)
        d = s / (hi - lo).astype(F32) - ext_ref[hl:hl + tm, cols]
        ys.append(_dot(d.astype(BF16), pw_ref[g]))
    pool = (jnp.concatenate(ys, axis=-1) * ps_ref[...]).astype(BF16)
    y = _dot(pool, wout_ref[0:POOL_DIM]) + _dot(attn_ref[0], wout_ref[POOL_DIM:])
    xn = x_ref[0] + g1_ref[0] * y
    _post(xn, g2n_ref, sh2_ref, sc2_ref, rwa_ref, rwb_ref, xo_ref, h2_ref, aff_ref)


def _mixout(x, zp, attn, pw, ps, wout, g1, g2n, sh2, sc2, rwa, rwb):
    B, N, D = x.shape
    tm = WIDE_TOKEN_TILE
    hb = tm // POOL_HALO
    nhb = N // POOL_HALO
    vec = lambda: pl.BlockSpec((1, 1, D), lambda b, i: (b, 0, 0))
    pin, pout = _post_specs(D, tm)
    return pl.pallas_call(
        _mixout_kernel,
        grid=(B, N // tm),
        in_specs=[pl.BlockSpec((1, tm, D), lambda b, i: (b, i, 0)),
                  pl.BlockSpec((1, tm, POOL_DIM), lambda b, i: (b, i, 0)),
                  pl.BlockSpec((1, POOL_HALO, POOL_DIM), lambda b, i: (b, jnp.maximum(i * hb - 1, 0), 0)),
                  pl.BlockSpec((1, POOL_HALO, POOL_DIM),
                               lambda b, i: (b, jnp.minimum((i + 1) * hb, nhb - 1), 0)),
                  pl.BlockSpec((1, tm, POOL_DIM), lambda b, i: (b, i, 0)),
                  pl.BlockSpec(pw.shape, lambda b, i: (0, 0, 0)),
                  pl.BlockSpec((1, POOL_DIM), lambda b, i: (0, 0)),
                  pl.BlockSpec(wout.shape, lambda b, i: (0, 0)),
                  vec()] + pin,
        out_specs=pout,
        out_shape=_post_shapes(B, N, D),
        scratch_shapes=[pltpu.VMEM((tm + 2 * POOL_HALO, POOL_DIM), F32)],
        compiler_params=_cparams(("parallel", "parallel")),
        name="mixout",
    )(x, zp, zp, zp, attn, pw, ps, wout, g1, g2n, sh2, sc2, rwa, rwb)


def _topk_kernel(aff_ref, rank_ref, rank_t_ref, gp_ref, wa_ref, wc_ref, as_ref):
    n = aff_ref.shape[1]
    nchunk = n // LANES

    a = aff_ref[0]
    big = _col_reduce(jnp.where(a >= SEARCH_FLOOR, 1.0, 0.0), jnp.sum) >= CAP
    as_ref[...] = a * jnp.where(big, 1.0, SEARCH_RESCALE)

    def enough(cand):
        return _col_reduce(jnp.where(as_ref[...] >= cand, 1.0, 0.0), jnp.sum) >= CAP

    p = jnp.full((1, LANES), SEARCH_FLOOR, F32)
    any_normal = enough(p)
    for shift in (32, 16, 8, 4, 2, 1):
        cand = p * (2.0 ** shift)
        p = jnp.where(enough(cand), cand, p)

    def mantissa(it, carry):
        t, step = carry
        cand = t + step
        return jnp.where(enough(cand), cand, t), step * 0.5

    thr, _ = lax.fori_loop(0, 23, mantissa, (p, p * 0.5))
    thr = jnp.where(any_normal, thr, 0.0)

    ri = lax.broadcasted_iota(jnp.int32, (LANES, LANES), 0)
    ci = lax.broadcasted_iota(jnp.int32, (LANES, LANES), 1)
    ltri = jnp.where(ri > ci, 1.0, 0.0).astype(BF16)

    def prefix(mask):
        run = jnp.zeros((1, LANES), F32)
        outs = []
        for c in range(nchunk):
            mc = mask[c * LANES:(c + 1) * LANES]
            outs.append(_dot(ltri, mc.astype(BF16)) + run)
            run = run + _col_reduce(mc, jnp.sum)
        return jnp.concatenate(outs, axis=0), run

    scaled = as_ref[...]
    gt = jnp.where(scaled > thr, 1.0, 0.0)
    eq = jnp.where(scaled == thr, 1.0, 0.0)
    need = CAP - _col_reduce(gt, jnp.sum)
    pe, _ = prefix(eq)
    sel = gt + eq * jnp.where(pe < need, 1.0, 0.0)
    pref, total = prefix(sel)
    rank = jnp.where(sel > 0.0, pref, -1.0)
    rank_ref[0] = rank
    for c in range(nchunk):
        rank_t_ref[0, :, c * LANES:(c + 1) * LANES] = rank[c * LANES:(c + 1) * LANES].T[:N_EXPERTS]
    gate = jnp.where(sel > 0.0, a, 0.0)
    g_hi = gate.astype(BF16)
    r1 = gate - g_hi.astype(F32)
    g_mid = r1.astype(BF16)
    g_lo = (r1 - g_mid.astype(F32)).astype(BF16)
    gp_ref[0] = jnp.concatenate([g_hi, g_mid, g_lo], axis=-1)
    lo =[pref[j * MOE_BLOCK:j * MOE_BLOCK + 1] for j in range(n // MOE_BLOCK)] + [total]
    wa, wc = [], []
    for j in range(n // MOE_BLOCK):
        start = jnp.minimum(jnp.floor(lo[j] * (1.0 / SLOT_ALIGN)) * SLOT_ALIGN, float(CAP - SLOT_WIN))
        wa.append(start)
        wc.append(start + jnp.where(lo[j + 1] > start + SLOT_WIN, WIN_OVERFLOW, 0.0))
    wa_ref[0] = jnp.concatenate(wa, axis=0)
    wc_ref[0] = jnp.concatenate(wc, axis=0)


def _topk(aff):
    B, N, E = aff.shape
    nj = N // MOE_BLOCK
    spec = lambda: pl.BlockSpec((1, N, E), lambda b: (b, 0, 0))
    wspec = lambda: pl.BlockSpec((1, nj, E), lambda b: (b, 0, 0))
    return pl.pallas_call(
        _topk_kernel,
        grid=(B,),
        in_specs=[spec()],
        out_specs=[spec(), pl.BlockSpec((1, N_EXPERTS, N), lambda b: (b, 0, 0)),
                   pl.BlockSpec((1, N, 3 * E), lambda b: (b, 0, 0)), wspec(), wspec()],
        out_shape=[jax.ShapeDtypeStruct((B, N, E), F32), jax.ShapeDtypeStruct((B, N_EXPERTS, N), F32),
                   jax.ShapeDtypeStruct((B, N, 3 * E), BF16),
                   jax.ShapeDtypeStruct((B, nj, E), F32), jax.ShapeDtypeStruct((B, nj, E), F32)],
        scratch_shapes=[pltpu.VMEM((N, E), F32)],
        compiler_params=_cparams(("parallel",)),
        name="topk",
    )(aff)


def _gather_kernel(wa_s, ovf_s, rank_t_ref, h2_ref, gp_ref, xs_ref, gs_ref):
    b = pl.program_id(0)
    jg = pl.program_id(1)
    tb = MOE_BLOCK
    nsub = h2_ref.shape[1] // tb

    @pl.when(jg == 0)
    def _():
        xs_ref[...] = jnp.zeros_like(xs_ref)
        gs_ref[...] = jnp.zeros_like(gs_ref)

    def gate_sum(g):
        return g[:, :LANES] + g[:, LANES:2 * LANES] + g[:, 2 * LANES:]

    sub = lax.broadcasted_iota(jnp.int32, (SLOT_WIN, tb), 0).astype(F32)

    def routing_block(sj):
        base = ((b * pl.num_programs(1) + jg) * nsub + sj) * N_EXPERTS
        tok = slice(sj * tb, (sj + 1) * tb)
        ps = []
        for e in range(N_EXPERTS):
            wcmp = (wa_s[base + e] + ovf_s[base + e] * int(WIN_OVERFLOW)).astype(F32)
            ps.append(jnp.where(rank_t_ref[0, e:e + 1, tok] - wcmp == sub, 1.0, 0.0).astype(BF16))
        pcat = jnp.concatenate(ps, axis=0)
        res = _dot(pcat, h2_ref[0, tok, :])
        gres = gate_sum(_dot(pcat, gp_ref[0, tok, :]))
        for e in range(N_EXPERTS):
            wa = pl.multiple_of(wa_s[base + e], SLOT_ALIGN)
            rows = slice(e * SLOT_WIN, (e + 1) * SLOT_WIN)
            xs_ref[0, e, pl.ds(wa, SLOT_WIN), :] += res[rows].astype(BF16)
            gs_ref[0, e, pl.ds(wa, SLOT_WIN), :] += gres[rows]

        def overflow(e, carry):
            @pl.when(ovf_s[base + e] != 0)
            def _():
                r = rank_t_ref[0, pl.ds(e, 1), tok]
                for k in range(CAP // SLOT_WIN):
                    p = jnp.where(r - float(k * SLOT_WIN) == sub, 1.0, 0.0).astype(BF16)
                    rows = slice(k * SLOT_WIN, (k + 1) * SLOT_WIN)
                    xs_ref[0, e, rows, :] += _dot(p, h2_ref[0, tok, :]).astype(BF16)
                    gs_ref[0, e, rows, :] += gate_sum(_dot(p, gp_ref[0, tok, :]))
            return carry

        lax.fori_loop(0, N_EXPERTS, overflow, 0)

    for sj in range(nsub):
        routing_block(sj)


def _gather(wa_s, ovf_s, rank_t, h2, gp):
    B, N, D = h2.shape
    E = N_EXPERTS
    tb = MOE_STEP
    return pl.pallas_call(
        _gather_kernel,
        grid_spec=pltpu.PrefetchScalarGridSpec(
            num_scalar_prefetch=2,
            grid=(B, N // tb),
            in_specs=[pl.BlockSpec((1, E, tb), lambda b, j, *_: (b, 0, j)),
                      pl.BlockSpec((1, tb, D), lambda b, j, *_: (b, j, 0)),
                      pl.BlockSpec((1, tb, 3 * LANES), lambda b, j, *_: (b, j, 0))],
            out_specs=[pl.BlockSpec((1, E, CAP, D), lambda b, j, *_: (b, 0, 0, 0)),
                       pl.BlockSpec((1, E, CAP, LANES), lambda b, j, *_: (b, 0, 0, 0))]),
        out_shape=[jax.ShapeDtypeStruct((B, E, CAP, D), BF16), jax.ShapeDtypeStruct((B, E, CAP, LANES), F32)],
        compiler_params=_cparams(("parallel", "arbitrary")),
        name="gather",
    )(wa_s, ovf_s, rank_t, h2, gp)


def _ffn_kernel(xs_ref, gs_ref, wg_ref, wu_ref, wd_ref, y_ref, wgb_ref, wub_ref, wdb_ref):
    e = pl.program_id(0)

    @pl.when(pl.program_id(1) == 0)
    def _():
        wgb_ref[...] = wg_ref[0, 0].astype(BF16)
        wub_ref[...] = wu_ref[0, 0].astype(BF16)
        wdb_ref[...] = wd_ref[0, 0].astype(BF16)

    lane = lax.broadcasted_iota(jnp.int32, gs_ref.shape[2:], 1)
    for s in range(xs_ref.shape[0]):
        xs = xs_ref[s, 0]
        a = _dot(xs, wgb_ref[...])
        u = _dot(xs, wub_ref[...])
        hm = (a * jax.nn.sigmoid(a) * u).astype(BF16)
        gate = jnp.sum(jnp.where(lane == e, gs_ref[s, 0], 0.0), axis=-1, keepdims=True)
        y_ref[s, 0] = (_dot(hm, wdb_ref[...]) * gate).astype(BF16)


def _ffn(xs, gs, wg, wu, wd, layer):
    B, E, C, D = xs.shape
    F = wg.shape[3]
    nb = FFN_BATCH
    return pl.pallas_call(
        _ffn_kernel,
        grid=(E, B // nb),
        in_specs=[pl.BlockSpec((nb, 1, C, D), lambda e, b: (b, e, 0, 0)),
                  pl.BlockSpec((nb, 1, C, LANES), lambda e, b: (b, e, 0, 0)),
                  pl.BlockSpec((1, 1, D, F), lambda e, b: (layer, e, 0, 0)),
                  pl.BlockSpec((1, 1, D, F), lambda e, b: (layer, e, 0, 0)),
                  pl.BlockSpec((1, 1, F, D), lambda e, b: (layer, e, 0, 0))],
        out_specs=pl.BlockSpec((nb, 1, C, D), lambda e, b: (b, e, 0, 0)),
        out_shape=jax.ShapeDtypeStruct((B, E, C, D), BF16),
        scratch_shapes=[pltpu.VMEM((D, F), BF16), pltpu.VMEM((D, F), BF16), pltpu.VMEM((F, D), BF16)],
        compiler_params=_cparams(("arbitrary", "arbitrary")),
        name="ffn",
    )(xs, gs, wg, wu, wd)


def _combine_kernel(wa_s, ovf_s, y_ref, rank_ref, wc_ref, spread_ref, x_ref, g2_ref, fg_ref, o_ref,
                    ycat_ref, acc_ref, *, final):
    b = pl.program_id(0)
    jg = pl.program_id(1)
    tb = MOE_BLOCK
    nsub = x_ref.shape[1] // tb
    kc = N_EXPERTS * SLOT_WIN

    def routing_block(sj):
        j = jg * nsub + sj
        base = (b * (pl.num_programs(1) * nsub) + j) * N_EXPERTS
        tok = slice(sj * tb, (sj + 1) * tb)
        for e in range(N_EXPERTS):
            wa = pl.multiple_of(wa_s[base + e], SLOT_ALIGN)
            ycat_ref[e * SLOT_WIN:(e + 1) * SLOT_WIN, :] = y_ref[0, e, pl.ds(wa, SLOT_WIN), :]
        rel = rank_ref[0, tok, :] - wc_ref[0, pl.ds(j, 1), :]
        rel = jnp.where(rel >= 0.0, jnp.where(rel < float(SLOT_WIN), rel, float(SLOT_WIN)), float(SLOT_WIN))
        spread = _dot(rel.astype(BF16), spread_ref[...])
        col = (lax.broadcasted_iota(jnp.int32, (tb, kc), 1) % SLOT_WIN).astype(F32)
        pt = jnp.where(spread == col, 1.0, 0.0).astype(BF16)
        acc_ref[...] = _dot(pt, ycat_ref[...])

        def overflow(e, carry):
            @pl.when(ovf_s[base + e] != 0)
            def _():
                lane = lax.broadcasted_iota(jnp.int32, (tb, LANES), 1)
                rk = jnp.sum(jnp.where(lane == e, rank_ref[0, tok, :], 0.0), axis=-1, keepdims=True)
                slot = lax.broadcasted_iota(jnp.int32, (tb, SLOT_WIN), 1).astype(F32)
                for k in range(CAP // SLOT_WIN):
                    p = jnp.where(rk - float(k * SLOT_WIN) == slot, 1.0, 0.0).astype(BF16)
                    acc_ref[...] += _dot(p, y_ref[0, e, k * SLOT_WIN:(k + 1) * SLOT_WIN, :])
            return carry

        lax.fori_loop(0, N_EXPERTS, overflow, 0)

        xn = x_ref[0, tok, :] + g2_ref[0] * acc_ref[...]
        if final:
            xn = _rms(xn, fg_ref[...])
        o_ref[0, tok, :] = xn

    for sj in range(nsub):
        routing_block(sj)


def _combine(wa_s, ovf_s, y, rank, wc, x, g2, fg, final):
    B, N, D = x.shape
    E = N_EXPERTS
    tb = MOE_STEP
    kc = E * SLOT_WIN
    spread = (jnp.arange(LANES)[:, None] == jnp.arange(kc)[None, :] // SLOT_WIN).astype(BF16)
    return pl.pallas_call(
        functools.partial(_combine_kernel, final=final),
        grid_spec=pltpu.PrefetchScalarGridSpec(
            num_scalar_prefetch=2,
            grid=(B, N // tb),
            in_specs=[pl.BlockSpec((1, E, CAP, D), lambda b, j, *_: (b, 0, 0, 0)),
                      pl.BlockSpec((1, tb, LANES), lambda b, j, *_: (b, j, 0)),
                      pl.BlockSpec((1, N // MOE_BLOCK, LANES), lambda b, j, *_: (b, 0, 0)),
                      pl.BlockSpec((LANES, kc), lambda b, j, *_: (0, 0)),
                      pl.BlockSpec((1, tb, D), lambda b, j, *_: (b, j, 0)),
                      pl.BlockSpec((1, 1, D), lambda b, j, *_: (b, 0, 0)),
                      pl.BlockSpec((1, D), lambda b, j, *_: (0, 0))],
            out_specs=pl.BlockSpec((1, tb, D), lambda b, j, *_: (b, j, 0)),
            scratch_shapes=[pltpu.VMEM((kc, D), BF16), pltpu.VMEM((MOE_BLOCK, D), F32)]),
        out_shape=jax.ShapeDtypeStruct((B, N, D), F32),
        compiler_params=_cparams(("parallel", "arbitrary")),
        name="combine",
    )(wa_s, ovf_s, y, rank, wc, spread, x, g2, fg)


def _moe(x, h2, aff, g2, wg, wu, wd, layer, fg, final):
    rank, rank_t, gp, wa, wc = _topk(aff)
    wa_s = wa[:, :, :N_EXPERTS].astype(jnp.int32).reshape(-1)
    ovf_s = (wc[:, :, :N_EXPERTS] != wa[:, :, :N_EXPERTS]).astype(jnp.int32).reshape(-1)
    xs, gs = _gather(wa_s, ovf_s, rank_t, h2, gp)
    y = _ffn(xs, gs, wg, wu, wd, layer)
    return _combine(wa_s, ovf_s, y, rank, wc, x, g2, fg, final)


def _pw1_kernel(x_ref, sh_ref, sc_ref, g_ref, w_ref, b_ref, u_ref):
    d = x_ref.shape[2]
    h = _rms(x_ref[0], g_ref[...]) * (1.0 + sc_ref[0]) + sh_ref[0]
    z = _dot(h.astype(BF16), w_ref[...]) + b_ref[...]
    u_ref[0] = z[:, :d] * jax.nn.sigmoid(z[:, d:])


def _pw1(x, sh, sc, g, w, b):
    B, N, D = x.shape
    tm = WIDE_TOKEN_TILE
    vec = lambda: pl.BlockSpec((1, 1, D), lambda b, i: (b, 0, 0))
    return pl.pallas_call(
        _pw1_kernel,
        grid=(B, N // tm),
        in_specs=[pl.BlockSpec((1, tm, D), lambda b, i: (b, i, 0)), vec(), vec(),
                  pl.BlockSpec((1, D), lambda b, i: (0, 0)),
                  pl.BlockSpec((D, 2 * D), lambda b, i: (0, 0)),
                  pl.BlockSpec((1, 2 * D), lambda b, i: (0, 0))],
        out_specs=pl.BlockSpec((1, tm, D), lambda b, i: (b, i, 0)),
        out_shape=jax.ShapeDtypeStruct((B, N, D), F32),
        compiler_params=_cparams(("parallel", "parallel")),
        name="pw1",
    )(x, sh, sc, g, w, b)


def _conv_kernel(x_ref, u_ref, uprev_ref, unext_ref, wdw_ref, bdw_ref, lng_ref, lnb_ref, w2_ref, b2_ref,
                 g1_ref, g2n_ref, sh2_ref, sc2_ref, rwa_ref, rwb_ref, xo_ref, h2_ref, aff_ref, ext_ref, cv_ref):
    i = pl.program_id(1)
    nt = pl.num_programs(1)
    tm = x_ref.shape[1]
    hl = CONV_HALO
    ext_ref[0:hl] = jnp.where(i > 0, uprev_ref[0], 0.0)
    ext_ref[hl:hl + tm] = u_ref[0]
    ext_ref[hl + tm:] = jnp.where(i < nt - 1, unext_ref[0], 0.0)
    lead = hl - CONV_WIDTH // 2

    def conv_rows(rb, carry):
        r0 = pl.multiple_of(rb * CONV_ROWS, CONV_ROWS)
        for lt in range(x_ref.shape[2] // LANES):
            cols = slice(lt * LANES, (lt + 1) * LANES)
            blk = ext_ref[pl.ds(r0, CONV_ROWS + 2 * hl), cols]
            acc = jnp.zeros((CONV_ROWS, LANES), F32)
            for r in range(8):
                view = pltpu.roll(blk, blk.shape[0] - (lead + r), 0)
                for a in range((CONV_WIDTH - r + 7) // 8):
                    k = 8 * a + r
                    acc = acc + view[8 * a:8 * a + CONV_ROWS] * wdw_ref[k:k + 1, cols]
            cv_ref[pl.ds(r0, CONV_ROWS), cols] = acc
        return carry

    lax.fori_loop(0, tm // CONV_ROWS, conv_rows, 0)
    acc = cv_ref[...] + bdw_ref[...]
    mu = jnp.mean(acc, axis=-1, keepdims=True)
    cen = acc - mu
    var = jnp.mean(cen * cen, axis=-1, keepdims=True)
    yn = cen * lax.rsqrt(var + EPS) * lng_ref[...] + lnb_ref[...]
    yn = yn * jax.nn.sigmoid(yn)
    y = _dot(yn.astype(BF16), w2_ref[...]) + b2_ref[...]
    xn = x_ref[0] + g1_ref[0] * y
    _post(xn, g2n_ref, sh2_ref, sc2_ref, rwa_ref, rwb_ref, xo_ref, h2_ref, aff_ref)


def _conv(x, u, wdw, bdw, lng, lnb, w2, b2, g1, g2n, sh2, sc2, rwa, rwb):
    B, N, D = x.shape
    tm = TOKEN_TILE
    hb = tm // CONV_HALO
    nhb = N // CONV_HALO
    vec = lambda: pl.BlockSpec((1, 1, D), lambda b, i: (b, 0, 0))
    row = lambda: pl.BlockSpec((1, D), lambda b, i: (0, 0))
    pin, pout = _post_specs(D, tm)
    return pl.pallas_call(
        _conv_kernel,
        grid=(B, N // tm),
        in_specs=[pl.BlockSpec((1, tm, D), lambda b, i: (b, i, 0)),
                  pl.BlockSpec((1, tm, D), lambda b, i: (b, i, 0)),
                  pl.BlockSpec((1, CONV_HALO, D), lambda b, i: (b, jnp.maximum(i * hb - 1, 0), 0)),
                  pl.BlockSpec((1, CONV_HALO, D), lambda b, i: (b, jnp.minimum((i + 1) * hb, nhb - 1), 0)),
                  pl.BlockSpec(wdw.shape, lambda b, i: (0, 0)),
                  row(), row(), row(),
                  pl.BlockSpec((D, D), lambda b, i: (0, 0)),
                  row(), vec()] + pin,
        out_specs=pout,
        out_shape=_post_shapes(B, N, D),
        scratch_shapes=[pltpu.VMEM((tm + 2 * CONV_HALO, D), F32), pltpu.VMEM((tm, D), F32)],
        compiler_params=_cparams(("parallel", "parallel")),
        name="conv",
    )(x, u, u, u, wdw, bdw, lng, lnb, w2, b2, g1, g2n, sh2, sc2, rwa, rwb)


def _rope_tables(n):
    rows = n // GRID_W
    row = jnp.repeat(jnp.arange(rows), GRID_W).astype(F32)
    col = jnp.tile(jnp.arange(GRID_W), rows).astype(F32)
    per_axis = QK_ROPE // 2
    inv_freq = 1.0 / (ROPE_THETA ** (jnp.arange(0, per_axis, 2, dtype=F32) / per_axis))
    ang = jnp.stack([row[:, None] * inv_freq, col[:, None] * inv_freq], axis=1)
    cos, sin = jnp.cos(ang), jnp.sin(ang)
    cos32 = jnp.broadcast_to(cos[:, :, None, :], (n, 2, 2, QK_ROPE // 4)).reshape(n, QK_ROPE)
    sin32 = jnp.stack([-sin, sin], axis=2).reshape(n, QK_ROPE)
    return cos32, sin32


def _head_tables(cos32, sin32, lead, scale):
    n = cos32.shape[0]
    cosf = jnp.concatenate([jnp.full((n, QK_NOPE), lead, F32), cos32, jnp.zeros((n, QK_ROPE), F32)], axis=1)
    sinf = jnp.concatenate([jnp.zeros((n, QK_NOPE), F32), sin32, jnp.zeros((n, QK_ROPE), F32)], axis=1)
    return cosf * scale, sinf * scale


def _layer0_weights(mix_w_in, w_uq, w_ukv):
    D = mix_w_in.shape[0]
    perm = jnp.arange(QK_ROPE) ^ (QK_ROPE // 4)
    rope_cols = mix_w_in[:, KV_OFF + KV_LORA:]
    wmix = jnp.concatenate([mix_w_in[:, :KV_OFF + KV_LORA], jnp.zeros((D, QK_NOPE), F32),
                            rope_cols, rope_cols[:, perm]], axis=1).astype(BF16)
    wq3 = w_uq.reshape(Q_LORA, MLA_HEADS, QK_DIM)
    wq = jnp.concatenate([wq3, wq3[:, :, QK_NOPE:][:, :, perm]], axis=2).reshape(Q_LORA, MLA_HEADS * HEAD_PAD)
    wkv3 = w_ukv.reshape(KV_LORA, MLA_HEADS, QK_NOPE + V_DIM)
    zpad = jnp.zeros((KV_LORA, MLA_HEADS, HEAD_PAD - QK_NOPE), F32)
    wk = jnp.concatenate([wkv3[:, :, :QK_NOPE], zpad], axis=2).reshape(KV_LORA, MLA_HEADS * HEAD_PAD)
    wvt = wkv3[:, :, QK_NOPE:].reshape(KV_LORA, MLA_HEADS * V_DIM).T
    return wmix, wq.astype(BF16), wk.astype(BF16), wvt.astype(BF16)


def _router_weights(rw):
    D, E = rw.shape
    rwp = jnp.concatenate([rw, jnp.zeros((D, LANES - E), F32)], axis=1)
    hi = rwp.astype(BF16)
    lo = (rwp - hi.astype(F32)).astype(BF16)
    return jnp.concatenate([hi, lo], axis=1), hi


def kernel(x, c, ctx, c_ctx, ada_w, ada_b, norm1_g, norm2_g, mix_w_in, pool_w, pool_scale, q_norm_g, kv_norm_g, w_uq, w_ukv, mix_w_out, conv_w_pw1, conv_b_pw1, conv_w_dw, conv_b_dw, conv_ln_g, conv_ln_b, conv_w_pw2, conv_b_pw2, router_w, exp_wg, exp_wu, exp_wd, final_g):
    B, N, D = x.shape
    LC = ctx.shape[1]

    cc = jnp.concatenate([c, c_ctx[None, :], jnp.zeros((16 - B - 1, D), F32)], axis=0)
    mods = _ada(cc, ada_w, ada_b)

    def mod(l, k):
        return mods[l, :B, k * D:(k + 1) * D].reshape(B, 1, D)

    def mod_ctx(l, k):
        return jnp.broadcast_to(mods[l, B, k * D:(k + 1) * D].reshape(1, 1, D), (B, 1, D))

    row = lambda v: v.reshape(1, -1)

    wmix, wq, wk, wvt = _layer0_weights(mix_w_in[0], w_uq[0], w_ukv[0])
    qscale = (QK_DIM ** -0.5) * math.log2(math.e)
    cos32, sin32 = _rope_tables(N)
    cq, sq = _head_tables(cos32, sin32, 1.0, qscale)
    ck, sk = _head_tables(cos32, sin32, 0.0, 1.0)
    one32, zero32 = jnp.ones((LC, QK_ROPE), F32), jnp.zeros((LC, QK_ROPE), F32)
    cqc, sqc = _head_tables(one32, zero32, 1.0, qscale)
    ckc, skc = _head_tables(one32, zero32, 0.0, 1.0)
    lw = (row(norm1_g[0]), wmix, row(q_norm_g[0]), wq, row(kv_norm_g[0]), wk, wvt)
    q, k, vt, zp = _mixin(x, mod(0, 0), mod(0, 1), *lw, cq, sq, ck, sk, WIDE_TOKEN_TILE)
    _, kc, vtc, _ = _mixin(ctx, mod_ctx(0, 0), mod_ctx(0, 1), *lw, cqc, sqc, ckc, skc, LC)
    attn = _attn(q, k, vt, kc, vtc)
    rwa, rwb = _router_weights(router_w[0])
    x, h2, aff = _mixout(x, zp, attn, pool_w[0].astype(BF16), row(pool_scale[0]), mix_w_out[0].astype(BF16),
                         mod(0, 2), row(norm2_g[0]), mod(0, 3), mod(0, 4), rwa, rwb)
    x = _moe(x, h2, aff, mod(0, 5), exp_wg, exp_wu, exp_wd, 0, row(final_g), False)

    u = _pw1(x, mod(1, 0), mod(1, 1), row(norm1_g[1]), conv_w_pw1[0].astype(BF16), row(conv_b_pw1[0]))
    wdw = jnp.concatenate([conv_w_dw[0], jnp.zeros((1, D), F32)], axis=0)
    rwa, rwb = _router_weights(router_w[1])
    x, h2, aff = _conv(x, u, wdw, row(conv_b_dw[0]), row(conv_ln_g[0]), row(conv_ln_b[0]),
                       conv_w_pw2[0].astype(BF16), row(conv_b_pw2[0]),
                       mod(1, 2), row(norm2_g[1]), mod(1, 3), mod(1, 4), rwa, rwb)
    return _moe(x, h2, aff, mod(1, 5), exp_wg, exp_wu, exp_wd, 1, row(final_g), True)
```

```python
import functools
import math

import jax
import jax.numpy as jnp
from jax import lax
from jax.experimental import pallas as pl
from jax.experimental.pallas import tpu as pltpu

F32 = jnp.float32
BF16 = jnp.bfloat16

D_MODEL = 1024
BATCH = 8
SEQ = 4096
CTX_LEN = 256
GRID_W = 64
POOL_WINDOWS = (2, 4, 8, 16)
POOL_GROUP_DIM = 128
POOL_DIM = 512
MLA_HEADS = 8
QK_NOPE = 64
QK_ROPE = 32
V_DIM = 64
Q_LORA = 256
KV_LORA = 128
QK_DIM = QK_NOPE + QK_ROPE
KV_OFF = POOL_DIM + Q_LORA
ROPE_THETA = 10000.0
CONV_WIDTH = 31
N_EXPERTS = 16
CAP = 2 * SEQ // N_EXPERTS
EPS = 1e-6

LANES = 128
HEAD_PAD = 128
POOL_HALO = 8
CONV_HALO = 16
CONV_ROWS = 64
TOKEN_TILE = 1024
WIDE_TOKEN_TILE = 1024
ATTN_HEADS_PER_STEP = 8
Q_TILE = 512
ATTN_KEY_CHUNK = 512
MOE_BLOCK = 256
MOE_STEP = 1024
GATHER_STEP = 1024
SLOT_WIN = 64
SLOT_ALIGN = 16
WIN_OVERFLOW = 4096.0
SEARCH_FLOOR = 2.0 ** -62
SEARCH_RESCALE = 2.0 ** 64
FFN_BATCH = 4
VMEM_LIMIT = 56 * 1024 * 1024


def _cparams(sem):
    return pltpu.CompilerParams(dimension_semantics=sem, vmem_limit_bytes=VMEM_LIMIT)


def _rms(xf, g):
    ms = jnp.mean(xf * xf, axis=-1, keepdims=True)
    return xf * lax.rsqrt(ms + EPS) * g


def _dot(a, b):
    return jnp.dot(a, b, preferred_element_type=F32)


def _dot_nt(a, b):
    return lax.dot_general(a, b, (((1,), (1,)), ((), ())), preferred_element_type=F32)


def _ada_kernel(c_ref, w_ref, b_ref, o_ref):
    cv = c_ref[...]
    s = cv * jax.nn.sigmoid(cv)
    o_ref[0] = jnp.dot(s, w_ref[0], precision=lax.Precision.HIGHEST,
                       preferred_element_type=F32) + b_ref[0]


def _ada(cc, ada_w, ada_b):
    L, D, D6 = ada_w.shape
    tn = 1536
    return pl.pallas_call(
        _ada_kernel,
        grid=(L, D6 // tn),
        in_specs=[pl.BlockSpec((16, D), lambda l, j: (0, 0)),
                  pl.BlockSpec((1, D, tn), lambda l, j: (l, 0, j)),
                  pl.BlockSpec((1, 1, tn), lambda l, j: (l, 0, j))],
        out_specs=pl.BlockSpec((1, 16, tn), lambda l, j: (l, 0, j)),
        out_shape=jax.ShapeDtypeStruct((L, 16, D6), F32),
        compiler_params=_cparams(("parallel", "parallel")),
        name="ada",
    )(cc, ada_w, ada_b.reshape(L, 1, D6))


def _mixin_kernel(x_ref, sh_ref, sc_ref, g_ref, wmix_ref, qg_ref, wq_ref, kvg_ref, wk_ref, wvt_ref,
                  cq_ref, sq_ref, ck_ref, sk_ref, q_ref, k_ref, vt_ref, zp_ref):
    h = _rms(x_ref[0], g_ref[...]) * (1.0 + sc_ref[0]) + sh_ref[0]
    z = _dot(h.astype(BF16), wmix_ref[...])
    zp_ref[0] = z[:, :POOL_DIM]
    zq = _rms(z[:, POOL_DIM:KV_OFF], qg_ref[...]).astype(BF16)
    qa = _dot(zq, wq_ref[...])
    cq = cq_ref[...]
    sq = sq_ref[...]
    for hh in range(MLA_HEADS):
        qh = qa[:, hh * HEAD_PAD:(hh + 1) * HEAD_PAD]
        q_ref[0, hh] = (qh * cq + pltpu.roll(qh, HEAD_PAD - QK_ROPE, 1) * sq).astype(BF16)
    zkv = _rms(z[:, KV_OFF:KV_OFF + KV_LORA], kvg_ref[...]).astype(BF16)
    ka = _dot(zkv, wk_ref[...])
    zr = z[:, KV_OFF + KV_LORA:]
    kr = zr * ck_ref[...] + pltpu.roll(zr, HEAD_PAD - QK_ROPE, 1) * sk_ref[...]
    for hh in range(MLA_HEADS):
        k_ref[0, hh] = (ka[:, hh * HEAD_PAD:(hh + 1) * HEAD_PAD] + kr).astype(BF16)
    vt_ref[0] = _dot_nt(wvt_ref[...], zkv).astype(BF16)


def _mixin(x, sh, sc, g, wmix, qg, wq, kvg, wk, wvt, cq, sq, ck, sk, tm):
    B, N, D = x.shape
    H = MLA_HEADS
    vec = lambda: pl.BlockSpec((1, 1, D), lambda b, i: (b, 0, 0))
    full = lambda a: pl.BlockSpec(a.shape, lambda b, i: (0,) * a.ndim)
    tab = lambda: pl.BlockSpec((tm, LANES), lambda b, i: (i, 0))
    hd = lambda: pl.BlockSpec((1, H, tm, HEAD_PAD), lambda b, i: (b, 0, i, 0))
    return pl.pallas_call(
        _mixin_kernel,
        grid=(B, N // tm),
        in_specs=[pl.BlockSpec((1, tm, D), lambda b, i: (b, i, 0)), vec(), vec(), full(g), full(wmix),
                  full(qg), full(wq), full(kvg), full(wk), full(wvt), tab(), tab(), tab(), tab()],
        out_specs=[hd(), hd(), pl.BlockSpec((1, H * V_DIM, tm), lambda b, i: (b, 0, i)),
                   pl.BlockSpec((1, tm, POOL_DIM), lambda b, i: (b, i, 0))],
        out_shape=[jax.ShapeDtypeStruct((B, H, N, HEAD_PAD), BF16)] * 2
        + [jax.ShapeDtypeStruct((B, H * V_DIM, N), BF16), jax.ShapeDtypeStruct((B, N, POOL_DIM), F32)],
        compiler_params=_cparams(("parallel", "parallel")),
        name="mixin",
    )(x, sh, sc, g, wmix, qg, wq, kvg, wk, wvt, cq, sq, ck, sk)


def _col_reduce(x, op):
    for group in (256, 64, 8):
        if x.shape[0] > group:
            x = op(x.reshape(x.shape[0] // group, group, x.shape[1]), axis=0)
    return op(x, axis=0, keepdims=True)


def _attn_kernel(q_ref, qn_ref, k_ref, vt_ref, kc_ref, vtc_ref, o_ref, s0_ref, s1_ref, m0_ref):
    tq = q_ref.shape[2]
    lc = kc_ref.shape[2]
    ck = ATTN_KEY_CHUNK
    nchunk = 1 + k_ref.shape[2] // ck

    def key_rows(c):
        return slice(0, lc) if c == 0 else slice(lc + (c - 1) * ck, lc + c * ck)

    def fill(dst_ref, qr, j, c, m):
        keys = kc_ref[0, j] if c == 0 else k_ref[0, j, (c - 1) * ck:c * ck, :]
        s = _dot_nt(keys, qr[0, j])
        dst_ref[key_rows(c), :] = s
        r = _col_reduce(s, jnp.max)
        return r if m is None else jnp.maximum(m, r)

    def consume(src_ref, j, c, m, acc, l):
        rows = slice(j * V_DIM, (j + 1) * V_DIM)
        vals = vtc_ref[0, rows, :] if c == 0 else vt_ref[0, rows, (c - 1) * ck:c * ck]
        p = jnp.exp2(src_ref[key_rows(c), :] - m)
        return acc + _dot(vals, p.astype(BF16)), l + _col_reduce(p, jnp.sum)

    nh = q_ref.shape[1]
    bufs = (s0_ref, s1_ref)

    @pl.when(pl.program_id(2) == 0)
    def _():
        m = None
        for c in range(nchunk):
            m = fill(s0_ref, q_ref, 0, c, m)
        m0_ref[...] = m

    m_cur = m0_ref[...]
    outs = []
    for h in range(nh):
        nxt_q, nxt_h = (q_ref, h + 1) if h + 1 < nh else (qn_ref, 0)
        m_nxt = None
        acc, l = jnp.zeros((V_DIM, tq), F32), jnp.zeros((1, tq), F32)
        for c in range(nchunk):
            m_nxt = fill(bufs[(h + 1) % 2], nxt_q, nxt_h, c, m_nxt)
            acc, l = consume(bufs[h % 2], h, c, m_cur, acc, l)
        outs.append(acc / l)
        m_cur = m_nxt
    m0_ref[...] = m_cur
    o_ref[0] = jnp.concatenate(outs, axis=0).T.astype(BF16)


def _attn(q, k, vt, kc, vtc):
    B, H, N, P = q.shape
    LC = kc.shape[2]
    tq = Q_TILE
    nq = N // tq
    nh = ATTN_HEADS_PER_STEP
    kspec = lambda n: pl.BlockSpec((1, nh, n, P), lambda b, p, i: (b, p, 0, 0))
    vspec = lambda n: pl.BlockSpec((1, nh * V_DIM, n), lambda b, p, i: (b, p, 0))
    return pl.pallas_call(
        _attn_kernel,
        grid=(B, H // nh, nq),
        in_specs=[pl.BlockSpec((1, nh, tq, P), lambda b, p, i: (b, p, i, 0)),
                  pl.BlockSpec((1, nh, tq, P), lambda b, p, i: (b, p, jnp.minimum(i + 1, nq - 1), 0)),
                  kspec(N), vspec(N), kspec(LC), vspec(LC)],
        out_specs=pl.BlockSpec((1, tq, nh * V_DIM), lambda b, p, i: (b, i, p)),
        out_shape=jax.ShapeDtypeStruct((B, N, MLA_HEADS * V_DIM), BF16),
        scratch_shapes=[pltpu.VMEM((LC + N, tq), F32), pltpu.VMEM((LC + N, tq), F32), pltpu.VMEM((1, tq), F32)],
        compiler_params=_cparams(("parallel", "parallel", "arbitrary")),
        name="attn",
    )(q, q, k, vt, kc, vtc)


def _post(xn, g2n_ref, sh2_ref, sc2_ref, rwa_ref, rwb_ref, xo_ref, h2_ref, aff_ref):
    xo_ref[0] = xn
    h2 = _rms(xn, g2n_ref[...]) * (1.0 + sc2_ref[0]) + sh2_ref[0]
    hi = h2.astype(BF16)
    lo = (h2 - hi.astype(F32)).astype(BF16)
    h2_ref[0] = hi
    la = _dot(hi, rwa_ref[...])
    lb = _dot(lo, rwb_ref[...])
    logits = la[:, :LANES] + la[:, LANES:] + lb
    lane = lax.broadcasted_iota(jnp.int32, logits.shape, 1)
    logits = jnp.where(lane < N_EXPERTS, logits, -1e30)
    m = jnp.max(logits, axis=-1, keepdims=True)
    e = jnp.exp(logits - m)
    aff_ref[0] = e / jnp.sum(e, axis=-1, keepdims=True)


def _post_specs(D, tm):
    vec = lambda: pl.BlockSpec((1, 1, D), lambda b, i: (b, 0, 0))
    in_specs = [pl.BlockSpec((1, D), lambda b, i: (0, 0)), vec(), vec(),
                pl.BlockSpec((D, 2 * LANES), lambda b, i: (0, 0)),
                pl.BlockSpec((D, LANES), lambda b, i: (0, 0))]
    out_specs = [pl.BlockSpec((1, tm, D), lambda b, i: (b, i, 0)),
                 pl.BlockSpec((1, tm, D), lambda b, i: (b, i, 0)),
                 pl.BlockSpec((1, tm, LANES), lambda b, i: (b, i, 0))]
    return in_specs, out_specs


def _post_shapes(B, N, D):
    return [jax.ShapeDtypeStruct((B, N, D), F32), jax.ShapeDtypeStruct((B, N, D), BF16),
            jax.ShapeDtypeStruct((B, N, LANES), F32)]


def _mixout_kernel(x_ref, zp_ref, zprev_ref, znext_ref, attn_ref, pw_ref, ps_ref, wout_ref, g1_ref,
                   g2n_ref, sh2_ref, sc2_ref, rwa_ref, rwb_ref, xo_ref, h2_ref, aff_ref, ext_ref):
    i = pl.program_id(1)
    nt = pl.num_programs(1)
    tm = x_ref.shape[1]
    hl = POOL_HALO
    ext_ref[0:hl] = jnp.where(i > 0, zprev_ref[0], 0.0)
    ext_ref[hl:hl + tm] = zp_ref[0]
    ext_ref[hl + tm:] = jnp.where(i < nt - 1, znext_ref[0], 0.0)
    t = i * tm + lax.broadcasted_iota(jnp.int32, (tm, 1), 0)
    ys = []
    for g, w in enumerate(POOL_WINDOWS):
        cols = slice(g * POOL_GROUP_DIM, (g + 1) * POOL_GROUP_DIM)
        s = ext_ref[:, cols]
        n = s.shape[0]
        span = 1
        while span < w:
            s = s + pltpu.roll(s, n - span, 0)
            span *= 2
        first = hl - w // 2
        s = (pltpu.roll(s, n - first, 0) if first else s)[:tm]
        lo = jnp.maximum(t - w // 2, 0)
        hi = jnp.minimum(t - w // 2 + w, SEQ)
        d = s / (hi - lo).astype(F32) - ext_ref[hl:hl + tm, cols]
        ys.append(_dot(d.astype(BF16), pw_ref[g]))
    pool = (jnp.concatenate(ys, axis=-1) * ps_ref[...]).astype(BF16)
    y = _dot(pool, wout_ref[0:POOL_DIM]) + _dot(attn_ref[0], wout_ref[POOL_DIM:])
    xn = x_ref[0] + g1_ref[0] * y
    _post(xn, g2n_ref, sh2_ref, sc2_ref, rwa_ref, rwb_ref, xo_ref, h2_ref, aff_ref)


def _mixout(x, zp, attn, pw, ps, wout, g1, g2n, sh2, sc2, rwa, rwb):
    B, N, D = x.shape
    tm = WIDE_TOKEN_TILE
    hb = tm // POOL_HALO
    nhb = N // POOL_HALO
    vec = lambda: pl.BlockSpec((1, 1, D), lambda b, i: (b, 0, 0))
    pin, pout = _post_specs(D, tm)
    return pl.pallas_call(
        _mixout_kernel,
        grid=(B, N // tm),
        in_specs=[pl.BlockSpec((1, tm, D), lambda b, i: (b, i, 0)),
                  pl.BlockSpec((1, tm, POOL_DIM), lambda b, i: (b, i, 0)),
                  pl.BlockSpec((1, POOL_HALO, POOL_DIM), lambda b, i: (b, jnp.maximum(i * hb - 1, 0), 0)),
                  pl.BlockSpec((1, POOL_HALO, POOL_DIM),
                               lambda b, i: (b, jnp.minimum((i + 1) * hb, nhb - 1), 0)),
                  pl.BlockSpec((1, tm, POOL_DIM), lambda b, i: (b, i, 0)),
                  pl.BlockSpec(pw.shape, lambda b, i: (0, 0, 0)),
                  pl.BlockSpec((1, POOL_DIM), lambda b, i: (0, 0)),
                  pl.BlockSpec(wout.shape, lambda b, i: (0, 0)),
                  vec()] + pin,
        out_specs=pout,
        out_shape=_post_shapes(B, N, D),
        scratch_shapes=[pltpu.VMEM((tm + 2 * POOL_HALO, POOL_DIM), F32)],
        compiler_params=_cparams(("parallel", "parallel")),
        name="mixout",
    )(x, zp, zp, zp, attn, pw, ps, wout, g1, g2n, sh2, sc2, rwa, rwb)


def _topk_kernel(aff_ref, rank_ref, rank_t_ref, gp_ref, wa_ref, wc_ref, as_ref):
    n = aff_ref.shape[1]
    nchunk = n // LANES

    a = aff_ref[0]
    big = _col_reduce(jnp.where(a >= SEARCH_FLOOR, 1.0, 0.0), jnp.sum) >= CAP
    as_ref[...] = a * jnp.where(big, 1.0, SEARCH_RESCALE)

    def enough(cand):
        return _col_reduce(jnp.where(as_ref[...] >= cand, 1.0, 0.0), jnp.sum) >= CAP

    p = jnp.full((1, LANES), SEARCH_FLOOR, F32)
    any_normal = enough(p)
    for shift in (32, 16, 8, 4, 2, 1):
        cand = p * (2.0 ** shift)
        p = jnp.where(enough(cand), cand, p)

    def mantissa(it, carry):
        t, step = carry
        cand = t + step
        return jnp.where(enough(cand), cand, t), step * 0.5

    thr, _ = lax.fori_loop(0, 23, mantissa, (p, p * 0.5))
    thr = jnp.where(any_normal, thr, 0.0)

    ri = lax.broadcasted_iota(jnp.int32, (LANES, LANES), 0)
    ci = lax.broadcasted_iota(jnp.int32, (LANES, LANES), 1)
    ltri = jnp.where(ri > ci, 1.0, 0.0).astype(BF16)

    def prefix(mask):
        run = jnp.zeros((1, LANES), F32)
        outs = []
        for c in range(nchunk):
            mc = mask[c * LANES:(c + 1) * LANES]
            outs.append(_dot(ltri, mc.astype(BF16)) + run)
            run = run + _col_reduce(mc, jnp.sum)
        return jnp.concatenate(outs, axis=0), run

    scaled = as_ref[...]
    gt = jnp.where(scaled > thr, 1.0, 0.0)
    eq = jnp.where(scaled == thr, 1.0, 0.0)
    need = CAP - _col_reduce(gt, jnp.sum)
    pe, _ = prefix(eq)
    sel = gt + eq * jnp.where(pe < need, 1.0, 0.0)
    pref, total = prefix(sel)
    rank = jnp.where(sel > 0.0, pref, -1.0)
    rank_ref[0] = rank
    for c in range(nchunk):
        rank_t_ref[0, :, c * LANES:(c + 1) * LANES] = rank[c * LANES:(c + 1) * LANES].T[:N_EXPERTS]
    gate = jnp.where(sel > 0.0, a, 0.0)
    g_hi = gate.astype(BF16)
    r1 = gate - g_hi.astype(F32)
    g_mid = r1.astype(BF16)
    g_lo = (r1 - g_mid.astype(F32)).astype(BF16)
    gp_ref[0] = jnp.concatenate([g_hi, g_mid, g_lo], axis=-1)
    lo =[pref[j * MOE_BLOCK:j * MOE_BLOCK + 1] for j in range(n // MOE_BLOCK)] + [total]
    wa, wc = [], []
    for j in range(n // MOE_BLOCK):
        start = jnp.minimum(jnp.floor(lo[j] * (1.0 / SLOT_ALIGN)) * SLOT_ALIGN, float(CAP - SLOT_WIN))
        wa.append(start)
        wc.append(start + jnp.where(lo[j + 1] > start + SLOT_WIN, WIN_OVERFLOW, 0.0))
    wa_ref[0] = jnp.concatenate(wa, axis=0)
    wc_ref[0] = jnp.concatenate(wc, axis=0)


def _topk(aff):
    B, N, E = aff.shape
    nj = N // MOE_BLOCK
    spec = lambda: pl.BlockSpec((1, N, E), lambda b: (b, 0, 0))
    wspec = lambda: pl.BlockSpec((1, nj, E), lambda b: (b, 0, 0))
    return pl.pallas_call(
        _topk_kernel,
        grid=(B,),
        in_specs=[spec()],
        out_specs=[spec(), pl.BlockSpec((1, N_EXPERTS, N), lambda b: (b, 0, 0)),
                   pl.BlockSpec((1, N, 3 * E), lambda b: (b, 0, 0)), wspec(), wspec()],
        out_shape=[jax.ShapeDtypeStruct((B, N, E), F32), jax.ShapeDtypeStruct((B, N_EXPERTS, N), F32),
                   jax.ShapeDtypeStruct((B, N, 3 * E), BF16),
                   jax.ShapeDtypeStruct((B, nj, E), F32), jax.ShapeDtypeStruct((B, nj, E), F32)],
        scratch_shapes=[pltpu.VMEM((N, E), F32)],
        compiler_params=_cparams(("parallel",)),
        name="topk",
    )(aff)


def _gather_kernel(wa_s, ovf_s, rank_t_ref, h2_ref, gp_ref, xs_ref, gs_ref):
    b = pl.program_id(0)
    jg = pl.program_id(1)
    tb = MOE_BLOCK
    nsub = h2_ref.shape[1] // tb

    @pl.when(jg == 0)
    def _():
        xs_ref[...] = jnp.zeros_like(xs_ref)
        gs_ref[...] = jnp.zeros_like(gs_ref)

    def gate_sum(g):
        return g[:, :LANES] + g[:, LANES:2 * LANES] + g[:, 2 * LANES:]

    sub = lax.broadcasted_iota(jnp.int32, (SLOT_WIN, tb), 0).astype(F32)

    def routing_block(sj):
        base = ((b * pl.num_programs(1) + jg) * nsub + sj) * N_EXPERTS
        tok = slice(sj * tb, (sj + 1) * tb)
        ps = []
        for e in range(N_EXPERTS):
            wcmp = (wa_s[base + e] + ovf_s[base + e] * int(WIN_OVERFLOW)).astype(F32)
            ps.append(jnp.where(rank_t_ref[0, e:e + 1, tok] - wcmp == sub, 1.0, 0.0).astype(BF16))
        pcat = jnp.concatenate(ps, axis=0)
        res = _dot(pcat, h2_ref[0, tok, :])
        gres = gate_sum(_dot(pcat, gp_ref[0, tok, :]))
        for e in range(N_EXPERTS):
            wa = pl.multiple_of(wa_s[base + e], SLOT_ALIGN)
            rows = slice(e * SLOT_WIN, (e + 1) * SLOT_WIN)
            xs_ref[0, e, pl.ds(wa, SLOT_WIN), :] += res[rows].astype(BF16)
            gs_ref[0, e, pl.ds(wa, SLOT_WIN), :] += gres[rows]

        def overflow(e, carry):
            @pl.when(ovf_s[base + e] != 0)
            def _():
                r = rank_t_ref[0, pl.ds(e, 1), tok]
                for k in range(CAP // SLOT_WIN):
                    p = jnp.where(r - float(k * SLOT_WIN) == sub, 1.0, 0.0).astype(BF16)
                    rows = slice(k * SLOT_WIN, (k + 1) * SLOT_WIN)
                    xs_ref[0, e, rows, :] += _dot(p, h2_ref[0, tok, :]).astype(BF16)
                    gs_ref[0, e, rows, :] += gate_sum(_dot(p, gp_ref[0, tok, :]))
            return carry

        lax.fori_loop(0, N_EXPERTS, overflow, 0)

    for sj in range(nsub):
        routing_block(sj)


def _gather(wa_s, ovf_s, rank_t, h2, gp):
    B, N, D = h2.shape
    E = N_EXPERTS
    tb = GATHER_STEP
    return pl.pallas_call(
        _gather_kernel,
        grid_spec=pltpu.PrefetchScalarGridSpec(
            num_scalar_prefetch=2,
            grid=(B, N // tb),
            in_specs=[pl.BlockSpec((1, E, tb), lambda b, j, *_: (b, 0, j)),
                      pl.BlockSpec((1, tb, D), lambda b, j, *_: (b, j, 0)),
                      pl.BlockSpec((1, tb, 3 * LANES), lambda b, j, *_: (b, j, 0))],
            out_specs=[pl.BlockSpec((1, E, CAP, D), lambda b, j, *_: (b, 0, 0, 0)),
                       pl.BlockSpec((1, E, CAP, LANES), lambda b, j, *_: (b, 0, 0, 0))]),
        out_shape=[jax.ShapeDtypeStruct((B, E, CAP, D), BF16), jax.ShapeDtypeStruct((B, E, CAP, LANES), F32)],
        compiler_params=_cparams(("parallel", "arbitrary")),
        name="gather",
    )(wa_s, ovf_s, rank_t, h2, gp)


def _ffn_kernel(xs_ref, gs_ref, wg_ref, wu_ref, wd_ref, y_ref, wgb_ref, wub_ref, wdb_ref):
    e = pl.program_id(0)

    @pl.when(pl.program_id(1) == 0)
    def _():
        wgb_ref[...] = wg_ref[0, 0].astype(BF16)
        wub_ref[...] = wu_ref[0, 0].astype(BF16)
        wdb_ref[...] = wd_ref[0, 0].astype(BF16)

    lane = lax.broadcasted_iota(jnp.int32, gs_ref.shape[2:], 1)
    for s in range(xs_ref.shape[0]):
        xs = xs_ref[s, 0]
        a = _dot(xs, wgb_ref[...])
        u = _dot(xs, wub_ref[...])
        hm = (a * jax.nn.sigmoid(a) * u).astype(BF16)
        gate = jnp.sum(jnp.where(lane == e, gs_ref[s, 0], 0.0), axis=-1, keepdims=True)
        y_ref[s, 0] = (_dot(hm, wdb_ref[...]) * gate).astype(BF16)


def _ffn(xs, gs, wg, wu, wd, layer):
    B, E, C, D = xs.shape
    F = wg.shape[3]
    nb = FFN_BATCH
    return pl.pallas_call(
        _ffn_kernel,
        grid=(E, B // nb),
        in_specs=[pl.BlockSpec((nb, 1, C, D), lambda e, b: (b, e, 0, 0)),
                  pl.BlockSpec((nb, 1, C, LANES), lambda e, b: (b, e, 0, 0)),
                  pl.BlockSpec((1, 1, D, F), lambda e, b: (layer, e, 0, 0)),
                  pl.BlockSpec((1, 1, D, F), lambda e, b: (layer, e, 0, 0)),
                  pl.BlockSpec((1, 1, F, D), lambda e, b: (layer, e, 0, 0))],
        out_specs=pl.BlockSpec((nb, 1, C, D), lambda e, b: (b, e, 0, 0)),
        out_shape=jax.ShapeDtypeStruct((B, E, C, D), BF16),
        scratch_shapes=[pltpu.VMEM((D, F), BF16), pltpu.VMEM((D, F), BF16), pltpu.VMEM((F, D), BF16)],
        compiler_params=_cparams(("arbitrary", "arbitrary")),
        name="ffn",
    )(xs, gs, wg, wu, wd)


def _combine_kernel(wa_s, ovf_s, y_ref, rank_ref, wc_ref, spread_ref, x_ref, g2_ref, fg_ref, o_ref,
                    ycat_ref, acc_ref, *, final):
    b = pl.program_id(0)
    jg = pl.program_id(1)
    tb = MOE_BLOCK
    nsub = x_ref.shape[1] // tb
    kc = N_EXPERTS * SLOT_WIN

    def routing_block(sj):
        j = jg * nsub + sj
        base = (b * (pl.num_programs(1) * nsub) + j) * N_EXPERTS
        tok = slice(sj * tb, (sj + 1) * tb)
        for e in range(N_EXPERTS):
            wa = pl.multiple_of(wa_s[base + e], SLOT_ALIGN)
            ycat_ref[e * SLOT_WIN:(e + 1) * SLOT_WIN, :] = y_ref[0, e, pl.ds(wa, SLOT_WIN), :]
        rel = rank_ref[0, tok, :] - wc_ref[0, pl.ds(j, 1), :]
        rel = jnp.where(rel >= 0.0, jnp.where(rel < float(SLOT_WIN), rel, float(SLOT_WIN)), float(SLOT_WIN))
        spread = _dot(rel.astype(BF16), spread_ref[...])
        col = (lax.broadcasted_iota(jnp.int32, (tb, kc), 1) % SLOT_WIN).astype(F32)
        pt = jnp.where(spread == col, 1.0, 0.0).astype(BF16)
        acc_ref[...] = _dot(pt, ycat_ref[...])

        def overflow(e, carry):
            @pl.when(ovf_s[base + e] != 0)
            def _():
                lane = lax.broadcasted_iota(jnp.int32, (tb, LANES), 1)
                rk = jnp.sum(jnp.where(lane == e, rank_ref[0, tok, :], 0.0), axis=-1, keepdims=True)
                slot = lax.broadcasted_iota(jnp.int32, (tb, SLOT_WIN), 1).astype(F32)
                for k in range(CAP // SLOT_WIN):
                    p = jnp.where(rk - float(k * SLOT_WIN) == slot, 1.0, 0.0).astype(BF16)
                    acc_ref[...] += _dot(p, y_ref[0, e, k * SLOT_WIN:(k + 1) * SLOT_WIN, :])
            return carry

        lax.fori_loop(0, N_EXPERTS, overflow, 0)

        xn = x_ref[0, tok, :] + g2_ref[0] * acc_ref[...]
        if final:
            xn = _rms(xn, fg_ref[...])
        o_ref[0, tok, :] = xn

    for sj in range(nsub):
        routing_block(sj)


def _combine(wa_s, ovf_s, y, rank, wc, x, g2, fg, final):
    B, N, D = x.shape
    E = N_EXPERTS
    tb = MOE_STEP
    kc = E * SLOT_WIN
    spread = (jnp.arange(LANES)[:, None] == jnp.arange(kc)[None, :] // SLOT_WIN).astype(BF16)
    return pl.pallas_call(
        functools.partial(_combine_kernel, final=final),
        grid_spec=pltpu.PrefetchScalarGridSpec(
            num_scalar_prefetch=2,
            grid=(B, N // tb),
            in_specs=[pl.BlockSpec((1, E, CAP, D), lambda b, j, *_: (b, 0, 0, 0)),
                      pl.BlockSpec((1, tb, LANES), lambda b, j, *_: (b, j, 0)),
                      pl.BlockSpec((1, N // MOE_BLOCK, LANES), lambda b, j, *_: (b, 0, 0)),
                      pl.BlockSpec((LANES, kc), lambda b, j, *_: (0, 0)),
                      pl.BlockSpec((1, tb, D), lambda b, j, *_: (b, j, 0)),
                      pl.BlockSpec((1, 1, D), lambda b, j, *_: (b, 0, 0)),
                      pl.BlockSpec((1, D), lambda b, j, *_: (0, 0))],
            out_specs=pl.BlockSpec((1, tb, D), lambda b, j, *_: (b, j, 0)),
            scratch_shapes=[pltpu.VMEM((kc, D), BF16), pltpu.VMEM((MOE_BLOCK, D), F32)]),
        out_shape=jax.ShapeDtypeStruct((B, N, D), F32),
        compiler_params=_cparams(("parallel", "arbitrary")),
        name="combine",
    )(wa_s, ovf_s, y, rank, wc, spread, x, g2, fg)


def _moe(x, h2, aff, g2, wg, wu, wd, layer, fg, final):
    rank, rank_t, gp, wa, wc = _topk(aff)
    wa_s = wa[:, :, :N_EXPERTS].astype(jnp.int32).reshape(-1)
    ovf_s = (wc[:, :, :N_EXPERTS] != wa[:, :, :N_EXPERTS]).astype(jnp.int32).reshape(-1)
    xs, gs = _gather(wa_s, ovf_s, rank_t, h2, gp)
    y = _ffn(xs, gs, wg, wu, wd, layer)
    return _combine(wa_s, ovf_s, y, rank, wc, x, g2, fg, final)


def _pw1_kernel(x_ref, sh_ref, sc_ref, g_ref, w_ref, b_ref, u_ref):
    d = x_ref.shape[2]
    h = _rms(x_ref[0], g_ref[...]) * (1.0 + sc_ref[0]) + sh_ref[0]
    z = _dot(h.astype(BF16), w_ref[...]) + b_ref[...]
    u_ref[0] = z[:, :d] * jax.nn.sigmoid(z[:, d:])


def _pw1(x, sh, sc, g, w, b):
    B, N, D = x.shape
    tm = WIDE_TOKEN_TILE
    vec = lambda: pl.BlockSpec((1, 1, D), lambda b, i: (b, 0, 0))
    return pl.pallas_call(
        _pw1_kernel,
        grid=(B, N // tm),
        in_specs=[pl.BlockSpec((1, tm, D), lambda b, i: (b, i, 0)), vec(), vec(),
                  pl.BlockSpec((1, D), lambda b, i: (0, 0)),
                  pl.BlockSpec((D, 2 * D), lambda b, i: (0, 0)),
                  pl.BlockSpec((1, 2 * D), lambda b, i: (0, 0))],
        out_specs=pl.BlockSpec((1, tm, D), lambda b, i: (b, i, 0)),
        out_shape=jax.ShapeDtypeStruct((B, N, D), F32),
        compiler_params=_cparams(("parallel", "parallel")),
        name="pw1",
    )(x, sh, sc, g, w, b)


def _conv_kernel(x_ref, u_ref, uprev_ref, unext_ref, wdw_ref, bdw_ref, lng_ref, lnb_ref, w2_ref, b2_ref,
                 g1_ref, g2n_ref, sh2_ref, sc2_ref, rwa_ref, rwb_ref, xo_ref, h2_ref, aff_ref, ext_ref, cv_ref):
    i = pl.program_id(1)
    nt = pl.num_programs(1)
    tm = x_ref.shape[1]
    hl = CONV_HALO
    ext_ref[0:hl] = jnp.where(i > 0, uprev_ref[0], 0.0)
    ext_ref[hl:hl + tm] = u_ref[0]
    ext_ref[hl + tm:] = jnp.where(i < nt - 1, unext_ref[0], 0.0)
    lead = hl - CONV_WIDTH // 2

    def conv_rows(rb, carry):
        r0 = pl.multiple_of(rb * CONV_ROWS, CONV_ROWS)
        for lt in range(x_ref.shape[2] // LANES):
            cols = slice(lt * LANES, (lt + 1) * LANES)
            blk = ext_ref[pl.ds(r0, CONV_ROWS + 2 * hl), cols]
            acc = jnp.zeros((CONV_ROWS, LANES), F32)
            for r in range(8):
                view = pltpu.roll(blk, blk.shape[0] - (lead + r), 0)
                for a in range((CONV_WIDTH - r + 7) // 8):
                    k = 8 * a + r
                    acc = acc + view[8 * a:8 * a + CONV_ROWS] * wdw_ref[k:k + 1, cols]
            cv_ref[pl.ds(r0, CONV_ROWS), cols] = acc
        return carry

    lax.fori_loop(0, tm // CONV_ROWS, conv_rows, 0)
    acc = cv_ref[...] + bdw_ref[...]
    mu = jnp.mean(acc, axis=-1, keepdims=True)
    cen = acc - mu
    var = jnp.mean(cen * cen, axis=-1, keepdims=True)
    yn = cen * lax.rsqrt(var + EPS) * lng_ref[...] + lnb_ref[...]
    yn = yn * jax.nn.sigmoid(yn)
    y = _dot(yn.astype(BF16), w2_ref[...]) + b2_ref[...]
    xn = x_ref[0] + g1_ref[0] * y
    _post(xn, g2n_ref, sh2_ref, sc2_ref, rwa_ref, rwb_ref, xo_ref, h2_ref, aff_ref)


def _conv(x, u, wdw, bdw, lng, lnb, w2, b2, g1, g2n, sh2, sc2, rwa, rwb):
    B, N, D = x.shape
    tm = TOKEN_TILE
    hb = tm // CONV_HALO
    nhb = N // CONV_HALO
    vec = lambda: pl.BlockSpec((1, 1, D), lambda b, i: (b, 0, 0))
    row = lambda: pl.BlockSpec((1, D), lambda b, i: (0, 0))
    pin, pout = _post_specs(D, tm)
    return pl.pallas_call(
        _conv_kernel,
        grid=(B, N // tm),
        in_specs=[pl.BlockSpec((1, tm, D), lambda b, i: (b, i, 0)),
                  pl.BlockSpec((1, tm, D), lambda b, i: (b, i, 0)),
                  pl.BlockSpec((1, CONV_HALO, D), lambda b, i: (b, jnp.maximum(i * hb - 1, 0), 0)),
                  pl.BlockSpec((1, CONV_HALO, D), lambda b, i: (b, jnp.minimum((i + 1) * hb, nhb - 1), 0)),
                  pl.BlockSpec(wdw.shape, lambda b, i: (0, 0)),
                  row(), row(), row(),
                  pl.BlockSpec((D, D), lambda b, i: (0, 0)),
                  row(), vec()] + pin,
        out_specs=pout,
        out_shape=_post_shapes(B, N, D),
        scratch_shapes=[pltpu.VMEM((tm + 2 * CONV_HALO, D), F32), pltpu.VMEM((tm, D), F32)],
        compiler_params=_cparams(("parallel", "parallel")),
        name="conv",
    )(x, u, u, u, wdw, bdw, lng, lnb, w2, b2, g1, g2n, sh2, sc2, rwa, rwb)


def _rope_tables(n):
    rows = n // GRID_W
    row = jnp.repeat(jnp.arange(rows), GRID_W).astype(F32)
    col = jnp.tile(jnp.arange(GRID_W), rows).astype(F32)
    per_axis = QK_ROPE // 2
    inv_freq = 1.0 / (ROPE_THETA ** (jnp.arange(0, per_axis, 2, dtype=F32) / per_axis))
    ang = jnp.stack([row[:, None] * inv_freq, col[:, None] * inv_freq], axis=1)
    cos, sin = jnp.cos(ang), jnp.sin(ang)
    cos32 = jnp.broadcast_to(cos[:, :, None, :], (n, 2, 2, QK_ROPE // 4)).reshape(n, QK_ROPE)
    sin32 = jnp.stack([-sin, sin], axis=2).reshape(n, QK_ROPE)
    return cos32, sin32


def _head_tables(cos32, sin32, lead, scale):
    n = cos32.shape[0]
    cosf = jnp.concatenate([jnp.full((n, QK_NOPE), lead, F32), cos32, jnp.zeros((n, QK_ROPE), F32)], axis=1)
    sinf = jnp.concatenate([jnp.zeros((n, QK_NOPE), F32), sin32, jnp.zeros((n, QK_ROPE), F32)], axis=1)
    return cosf * scale, sinf * scale


def _layer0_weights(mix_w_in, w_uq, w_ukv):
    D = mix_w_in.shape[0]
    perm = jnp.arange(QK_ROPE) ^ (QK_ROPE // 4)
    rope_cols = mix_w_in[:, KV_OFF + KV_LORA:]
    wmix = jnp.concatenate([mix_w_in[:, :KV_OFF + KV_LORA], jnp.zeros((D, QK_NOPE), F32),
                            rope_cols, rope_cols[:, perm]], axis=1).astype(BF16)
    wq3 = w_uq.reshape(Q_LORA, MLA_HEADS, QK_DIM)
    wq = jnp.concatenate([wq3, wq3[:, :, QK_NOPE:][:, :, perm]], axis=2).reshape(Q_LORA, MLA_HEADS * HEAD_PAD)
    wkv3 = w_ukv.reshape(KV_LORA, MLA_HEADS, QK_NOPE + V_DIM)
    zpad = jnp.zeros((KV_LORA, MLA_HEADS, HEAD_PAD - QK_NOPE), F32)
    wk = jnp.concatenate([wkv3[:, :, :QK_NOPE], zpad], axis=2).reshape(KV_LORA, MLA_HEADS * HEAD_PAD)
    wvt = wkv3[:, :, QK_NOPE:].reshape(KV_LORA, MLA_HEADS * V_DIM).T
    return wmix, wq.astype(BF16), wk.astype(BF16), wvt.astype(BF16)


def _router_weights(rw):
    D, E = rw.shape
    rwp = jnp.concatenate([rw, jnp.zeros((D, LANES - E), F32)], axis=1)
    hi = rwp.astype(BF16)
    lo = (rwp - hi.astype(F32)).astype(BF16)
    return jnp.concatenate([hi, lo], axis=1), hi


def kernel(x, c, ctx, c_ctx, ada_w, ada_b, norm1_g, norm2_g, mix_w_in, pool_w, pool_scale, q_norm_g, kv_norm_g, w_uq, w_ukv, mix_w_out, conv_w_pw1, conv_b_pw1, conv_w_dw, conv_b_dw, conv_ln_g, conv_ln_b, conv_w_pw2, conv_b_pw2, router_w, exp_wg, exp_wu, exp_wd, final_g):
    B, N, D = x.shape
    LC = ctx.shape[1]

    cc = jnp.concatenate([c, c_ctx[None, :], jnp.zeros((16 - B - 1, D), F32)], axis=0)
    mods = _ada(cc, ada_w, ada_b)

    def mod(l, k):
        return mods[l, :B, k * D:(k + 1) * D].reshape(B, 1, D)

    def mod_ctx(l, k):
        return jnp.broadcast_to(mods[l, B, k * D:(k + 1) * D].reshape(1, 1, D), (B, 1, D))

    row = lambda v: v.reshape(1, -1)

    wmix, wq, wk, wvt = _layer0_weights(mix_w_in[0], w_uq[0], w_ukv[0])
    qscale = (QK_DIM ** -0.5) * math.log2(math.e)
    cos32, sin32 = _rope_tables(N)
    cq, sq = _head_tables(cos32, sin32, 1.0, qscale)
    ck, sk = _head_tables(cos32, sin32, 0.0, 1.0)
    one32, zero32 = jnp.ones((LC, QK_ROPE), F32), jnp.zeros((LC, QK_ROPE), F32)
    cqc, sqc = _head_tables(one32, zero32, 1.0, qscale)
    ckc, skc = _head_tables(one32, zero32, 0.0, 1.0)
    lw = (row(norm1_g[0]), wmix, row(q_norm_g[0]), wq, row(kv_norm_g[0]), wk, wvt)
    q, k, vt, zp = _mixin(x, mod(0, 0), mod(0, 1), *lw, cq, sq, ck, sk, WIDE_TOKEN_TILE)
    _, kc, vtc, _ = _mixin(ctx, mod_ctx(0, 0), mod_ctx(0, 1), *lw, cqc, sqc, ckc, skc, LC)
    attn = _attn(q, k, vt, kc, vtc)
    rwa, rwb = _router_weights(router_w[0])
    x, h2, aff = _mixout(x, zp, attn, pool_w[0].astype(BF16), row(pool_scale[0]), mix_w_out[0].astype(BF16),
                         mod(0, 2), row(norm2_g[0]), mod(0, 3), mod(0, 4), rwa, rwb)
    x = _moe(x, h2, aff, mod(0, 5), exp_wg, exp_wu, exp_wd, 0, row(final_g), False)

    u = _pw1(x, mod(1, 0), mod(1, 1), row(norm1_g[1]), conv_w_pw1[0].astype(BF16), row(conv_b_pw1[0]))
    wdw = jnp.concatenate([conv_w_dw[0], jnp.zeros((1, D), F32)], axis=0)
    rwa, rwb = _router_weights(router_w[1])
    x, h2, aff = _conv(x, u, wdw, row(conv_b_dw[0]), row(conv_ln_g[0]), row(conv_ln_b[0]),
                       conv_w_pw2[0].astype(BF16), row(conv_b_pw2[0]),
                       mod(1, 2), row(norm2_g[1]), mod(1, 3), mod(1, 4), rwa, rwb)
    return _moe(x, h2, aff, mod(1, 5), exp_wg, exp_wu, exp_wd, 1, row(final_g), True)
```
